```python
import jax, jax.numpy as jnp
from jax import lax
import numpy as np

D_MODEL = 2048
BATCH = 4
SEQ = 8192
DEPTH = 2
DEC_BATCH = 8
DEC_SEQ = 16
PAST_LEN = 4096

CHUNK = 64
N_HEADS = 16
HEAD_DIM = D_MODEL // N_HEADS
LEFT_CHUNKS = 8
BAND_PAST = LEFT_CHUNKS * CHUNK
BAND = BAND_PAST + CHUNK
REL_MAX = 256
REL_MIN = -(CHUNK - 1)
N_REL = REL_MAX - REL_MIN + 1
POOL_WINDOWS = (2, 4, 8, 16)
N_POOL_GROUPS = len(POOL_WINDOWS)
POOL_GROUP_DIM = D_MODEL // N_POOL_GROUPS
POOL_HIST = max(POOL_WINDOWS) - 1
PEER_HEADS = 8
PEER_KEY_DIM = 256
PEER_HALF = PEER_KEY_DIM // 2
N_KEYS = 128
N_EXPERTS = N_KEYS * N_KEYS
PEER_TOPK = 16
PEER_TOK_BLOCK = 128
PLE_DIM = 256
ALPHA = (2 * DEPTH) ** 0.25
BETA = (8 * DEPTH) ** -0.25
N_ATT_LAYERS = (DEPTH + 1) // 2
N_POOL_LAYERS = DEPTH // 2
LN_EPS = 1e-5
NEG_INF = -1e30

kernel_name = 'chunk_band_pool_peer_stream_step'


def layer_norm(x, g, b):
    xf = x.astype(jnp.float32)
    mu = jnp.mean(xf, axis=-1, keepdims=True)
    var = jnp.mean(jnp.square(xf - mu), axis=-1, keepdims=True)
    y = (xf - mu) * lax.rsqrt(var + LN_EPS)
    return (y * g.astype(jnp.float32) + b.astype(jnp.float32)).astype(x.dtype)


def band_attention(x, hist_k, hist_v, pos0, w_qkv, w_o, rel_bias):
    B, S, _ = x.shape
    qkv = (x @ w_qkv).reshape(B, S, 3, N_HEADS, HEAD_DIM)
    q, k, v = qkv[:, :, 0], qkv[:, :, 1], qkv[:, :, 2]
    L = hist_k.shape[1]
    hpad = ((0, 0), (BAND_PAST - L, 0), (0, 0), (0, 0))
    hk = jnp.pad(hist_k, hpad)
    hv = jnp.pad(hist_v, hpad)
    n_chunks = -(-S // CHUNK)
    Sp = n_chunks * CHUNK
    tpad = ((0, 0), (0, Sp - S), (0, 0), (0, 0))
    q_p = jnp.pad(q, tpad)
    keys = jnp.concatenate([hk, jnp.pad(k, tpad)], axis=1)
    vals = jnp.concatenate([hv, jnp.pad(v, tpad)], axis=1)
    iq = np.arange(CHUNK)
    ik = np.arange(BAND)
    rel = np.clip(BAND_PAST + iq[:, None] - ik[None, :], REL_MIN, REL_MAX) - REL_MIN
    bias = rel_bias.astype(jnp.float32)[:, rel]
    scale = HEAD_DIM ** -0.5

    def one_chunk(c):
        qc = lax.dynamic_slice_in_dim(q_p, c * CHUNK, CHUNK, axis=1)
        kc = lax.dynamic_slice_in_dim(keys, c * CHUNK, BAND, axis=1)
        vc = lax.dynamic_slice_in_dim(vals, c * CHUNK, BAND, axis=1)
        kpos = pos0 - BAND_PAST + c * CHUNK + ik
        valid = (kpos >= 0) & (kpos < pos0 + S)
        s = jnp.einsum('bqhd,bkhd->bhqk', qc, kc).astype(jnp.float32) * scale + bias[None]
        s = jnp.where(valid[None, None, None, :], s, NEG_INF)
        p = jax.nn.softmax(s, axis=-1).astype(vc.dtype)
        return jnp.einsum('bhqk,bkhd->bqhd', p, vc)

    o = lax.map(one_chunk, jnp.arange(n_chunks))
    o = jnp.moveaxis(o, 0, 1).reshape(B, Sp, D_MODEL)[:, :S]
    return o @ w_o, k, v


def pool_mixer(x, hist, pos0, w_in, w_grp, scale):
    B, S, D = x.shape
    u = x @ w_in
    hist = jnp.pad(hist, ((0, 0), (POOL_HIST - hist.shape[1], 0), (0, 0)))
    ext = jnp.concatenate([hist, u], axis=1)
    cs = jnp.cumsum(ext.astype(jnp.float32), axis=1)
    cs = jnp.pad(cs, ((0, 0), (1, 0), (0, 0))).reshape(B, POOL_HIST + S + 1, N_POOL_GROUPS, POOL_GROUP_DIM)
    win = np.array(POOL_WINDOWS, np.int32)
    end = POOL_HIST + 1 + np.arange(S)
    start = end[:, None] - win[None, :]
    grp = np.arange(N_POOL_GROUPS)[None, :]
    win_sum = cs[:, end] - cs[:, start, grp]
    pos = pos0 + np.arange(S)
    count = np.minimum(win[None, :], pos[:, None] + 1).astype(np.float32)
    mean = win_sum / count[None, :, :, None]
    y = (mean - u.astype(jnp.float32).reshape(B, S, N_POOL_GROUPS, POOL_GROUP_DIM)).astype(x.dtype)
    z = jnp.einsum('bsgc,gcd->bsgd', y, w_grp).reshape(B, S, D) * scale
    return z, ext[:, -POOL_HIST:]


def peer_ffn(x, w_query, sub_keys, expert_u, expert_v):
    B, S, D = x.shape
    T = B * S
    n_blocks = -(-T // PEER_TOK_BLOCK)
    xf = jnp.pad(x.reshape(T, D), ((0, n_blocks * PEER_TOK_BLOCK - T), (0, 0)))
    xf = xf.reshape(n_blocks, PEER_TOK_BLOCK, D)
    sk = sub_keys.astype(jnp.float32)

    def block(xb):
        tb = xb.shape[0]
        q = (xb @ w_query).astype(jnp.float32).reshape(tb, PEER_HEADS, 2, PEER_HALF)
        s = jnp.einsum('thpc,hpnc->thpn', q, sk)
        s1, i1 = lax.top_k(s[:, :, 0], PEER_TOPK)
        s2, i2 = lax.top_k(s[:, :, 1], PEER_TOPK)
        cand = (s1[..., :, None] + s2[..., None, :]).reshape(tb, PEER_HEADS, PEER_TOPK * PEER_TOPK)
        sc, ci = lax.top_k(cand, PEER_TOPK)
        e = (jnp.take_along_axis(i1, ci // PEER_TOPK, axis=-1) * N_KEYS
             + jnp.take_along_axis(i2, ci % PEER_TOPK, axis=-1))
        g = jax.nn.softmax(sc, axis=-1).reshape(tb, PEER_HEADS * PEER_TOPK)
        e = e.reshape(tb, PEER_HEADS * PEER_TOPK)
        h = jax.nn.gelu(jnp.einsum('td,tkd->tk', xb, expert_u[e]), approximate=False)
        w = (g * h.astype(jnp.float32)).astype(xb.dtype)
        return jnp.einsum('tk,tkd->td', w, expert_v[e])

    y = lax.map(block, xf)
    return y.reshape(-1, D)[:T].reshape(B, S, D)


def per_layer_embedding(x, p, w_proj, w_gate):
    return jax.nn.sigmoid(x @ w_gate) * (p @ w_proj)


def setup_inputs(seed: int = 0) -> dict:
    key = jax.random.key(seed)
    ks = jax.random.split(key, 24)

    def nrm(k, shape, s):
        return jax.random.normal(k, shape, jnp.float32) * s

    D = D_MODEL
    cache_rows = min(BAND_PAST, PAST_LEN)
    w_qk = nrm(ks[7], (N_ATT_LAYERS, D, 2 * D), D ** -0.5)
    w_v = nrm(ks[8], (N_ATT_LAYERS, D, D), BETA * D ** -0.5)
    return {
        'x_prompt': nrm(ks[0], (BATCH, SEQ, D), 1.0),
        'x_sample': nrm(ks[1], (DEC_BATCH, DEC_SEQ, D), 1.0),
        'cache_att_k': nrm(ks[2], (N_ATT_LAYERS, DEC_BATCH, cache_rows, N_HEADS, HEAD_DIM), 1.0),
        'cache_att_v': nrm(ks[3], (N_ATT_LAYERS, DEC_BATCH, cache_rows, N_HEADS, HEAD_DIM), BETA),
        'state_pool': nrm(ks[4], (N_POOL_LAYERS, DEC_BATCH, POOL_HIST, D), 1.0),
        'p_prompt': nrm(ks[5], (DEPTH, BATCH, SEQ, PLE_DIM), 1.0),
        'p_sample': nrm(ks[6], (DEPTH, DEC_BATCH, DEC_SEQ, PLE_DIM), 1.0),
        'att_w_qkv': jnp.concatenate([w_qk, w_v], axis=-1),
        'att_w_o': nrm(ks[9], (N_ATT_LAYERS, D, D), BETA * D ** -0.5),
        'att_rel_bias': nrm(ks[10], (N_ATT_LAYERS, N_HEADS, N_REL), 0.5),
        'pool_w_in': nrm(ks[11], (N_POOL_LAYERS, D, D), D ** -0.5),
        'pool_w_grp': nrm(ks[12], (N_POOL_LAYERS, N_POOL_GROUPS, POOL_GROUP_DIM, POOL_GROUP_DIM), BETA * POOL_GROUP_DIM ** -0.5),
        'pool_scale': 1.0 + nrm(ks[13], (N_POOL_LAYERS, D), 0.1),
        'peer_w_query': nrm(ks[14], (DEPTH, D, PEER_HEADS * PEER_KEY_DIM), D ** -0.5),
        'peer_sub_keys': nrm(ks[15], (DEPTH, PEER_HEADS, 2, N_KEYS, PEER_HALF), PEER_HALF ** -0.5),
        'peer_u': nrm(ks[16], (DEPTH, N_EXPERTS, D), D ** -0.5),
        'peer_v': nrm(ks[17], (DEPTH, N_EXPERTS, D), BETA),
        'ln_g': 1.0 + nrm(ks[18], (DEPTH, 2, D), 0.02),
        'ln_b': nrm(ks[19], (DEPTH, 2, D), 0.02),
        'ple_w_proj': nrm(ks[20], (DEPTH, PLE_DIM, D), PLE_DIM ** -0.5),
        'ple_w_gate': nrm(ks[21], (DEPTH, D, D), D ** -0.5),
    }


def reference(x_prompt, x_sample, cache_att_k, cache_att_v, state_pool, p_prompt, p_sample,
              att_w_qkv, att_w_o, att_rel_bias, pool_w_in, pool_w_grp, pool_scale,
              peer_w_query, peer_sub_keys, peer_u, peer_v, ln_g, ln_b, ple_w_proj, ple_w_gate):
    xp, xs = x_prompt, x_sample
    Bp, Sp_len, D = xp.shape
    kp_out, vp_out, ks_out, vs_out, pp_out, ps_out = [], [], [], [], [], []
    for i in range(DEPTH):
        j = i // 2
        if i % 2 == 0:
            empty = jnp.zeros((Bp, 0, N_HEADS, HEAD_DIM), xp.dtype)
            hp, kp, vp = band_attention(xp, empty, empty, 0, att_w_qkv[j], att_w_o[j], att_rel_bias[j])
            hs, k_s, v_s = band_attention(xs, cache_att_k[j], cache_att_v[j], PAST_LEN,
                                          att_w_qkv[j], att_w_o[j], att_rel_bias[j])
            keep = min(BAND_PAST, Sp_len)
            kp_out.append(kp[:, -keep:])
            vp_out.append(vp[:, -keep:])
            ks_out.append(k_s)
            vs_out.append(v_s)
        else:
            empty = jnp.zeros((Bp, 0, D), xp.dtype)
            hp, sp = pool_mixer(xp, empty, 0, pool_w_in[j], pool_w_grp[j], pool_scale[j])
            hs, ss = pool_mixer(xs, state_pool[j], PAST_LEN, pool_w_in[j], pool_w_grp[j], pool_scale[j])
            pp_out.append(sp)
            ps_out.append(ss)
        xp = layer_norm(ALPHA * xp + hp, ln_g[i, 0], ln_b[i, 0])
        xs = layer_norm(ALPHA * xs + hs, ln_g[i, 0], ln_b[i, 0])
        xp = layer_norm(ALPHA * xp + peer_ffn(xp, peer_w_query[i], peer_sub_keys[i], peer_u[i], peer_v[i]), ln_g[i, 1], ln_b[i, 1])
        xs = layer_norm(ALPHA * xs + peer_ffn(xs, peer_w_query[i], peer_sub_keys[i], peer_u[i], peer_v[i]), ln_g[i, 1], ln_b[i, 1])
        xp = xp + per_layer_embedding(xp, p_prompt[i], ple_w_proj[i], ple_w_gate[i])
        xs = xs + per_layer_embedding(xs, p_sample[i], ple_w_proj[i], ple_w_gate[i])
    return (xp, xs, jnp.stack(kp_out), jnp.stack(vp_out), jnp.stack(ks_out), jnp.stack(vs_out),
            jnp.stack(pp_out), jnp.stack(ps_out))
```

```python
import functools

import jax
import jax.numpy as jnp
import numpy as np
from jax import lax
from jax.experimental import pallas as pl
from jax.experimental.pallas import tpu as pltpu

F32 = jnp.float32
BF16 = jnp.bfloat16

D_MODEL = 2048
DEPTH = 2
PAST_LEN = 4096
CHUNK = 64
N_HEADS = 16
HEAD_DIM = D_MODEL // N_HEADS
LEFT_CHUNKS = 8
BAND_PAST = LEFT_CHUNKS * CHUNK
BAND = BAND_PAST + CHUNK
REL_MAX = 256
REL_MIN = -(CHUNK - 1)
POOL_WINDOWS = (2, 4, 8, 16)
N_POOL_GROUPS = len(POOL_WINDOWS)
POOL_GROUP_DIM = D_MODEL // N_POOL_GROUPS
POOL_HIST = max(POOL_WINDOWS) - 1
POOL_CARRY = 16
PEER_HEADS = 8
PEER_KEY_DIM = 256
PEER_HALF = PEER_KEY_DIM // 2
N_KEYS = 128
N_EXPERTS = N_KEYS * N_KEYS
PEER_TOPK = 16
PLE_DIM = 256
PEER_ROW_CHUNK = 512
ALPHA = (2 * DEPTH) ** 0.25
LN_EPS = 1e-5
NEG_INF = -1e30

LANES = 128
SUBLANES = 8
VMEM_LIMIT_BYTES = 56 * 1024 * 1024


def _cparams(sem):
    return pltpu.CompilerParams(dimension_semantics=sem, vmem_limit_bytes=VMEM_LIMIT_BYTES)


def _layer_norm(z, g, b):
    mu = jnp.mean(z, axis=-1, keepdims=True)
    zc = z - mu
    var = jnp.mean(zc * zc, axis=-1, keepdims=True)
    return zc * lax.rsqrt(var + LN_EPS) * g + b


def _mm_kernel(x_ref, w_ref, o_ref):
    x = x_ref[...].astype(BF16)
    o_ref[...] = jnp.dot(x, w_ref[...], preferred_element_type=F32).astype(o_ref.dtype)


def _matmul(x, w, out_dtype, tm):
    t, k = x.shape
    n = w.shape[1]
    tn = min(n, D_MODEL)
    tm = min(tm, t)
    return pl.pallas_call(
        _mm_kernel,
        grid=(n // tn, t // tm),
        in_specs=[pl.BlockSpec((tm, k), lambda j, i: (i, 0)),
                  pl.BlockSpec((k, tn), lambda j, i: (0, j))],
        out_specs=pl.BlockSpec((tm, tn), lambda j, i: (i, j)),
        out_shape=jax.ShapeDtypeStruct((t, n), out_dtype),
        compiler_params=_cparams(("arbitrary", "arbitrary")),
        name="proj",
    )(x, w)


def _attn_kernel(q_ref, kp_ref, kc_ref, vp_ref, vc_ref, bias_ref, o_ref, kwin, vwin,
                 *, qb, sub, pos0, seq_len):
    i = pl.program_id(1)
    w = sub + BAND_PAST
    kwin[0:BAND_PAST, :] = kp_ref[0]
    kwin[BAND_PAST:BAND_PAST + qb, :] = kc_ref[0]
    vwin[0:BAND_PAST, :] = vp_ref[0]
    vwin[BAND_PAST:BAND_PAST + qb, :] = vc_ref[0]
    scale = HEAD_DIM ** -0.5
    col = lax.broadcasted_iota(jnp.int32, (sub, w), 1)

    def sub_body(s, carry):
        r0 = pl.multiple_of(s * sub, sub)
        kpos = (pos0 - BAND_PAST) + i * qb + r0 + col
        valid = (kpos >= 0) & (kpos < pos0 + seq_len)
        for h in range(N_HEADS):
            c0 = h * HEAD_DIM
            q = q_ref[0, pl.ds(r0, sub), c0:c0 + HEAD_DIM]
            k = kwin[pl.ds(r0, w), c0:c0 + HEAD_DIM]
            v = vwin[pl.ds(r0, w), c0:c0 + HEAD_DIM]
            sc = lax.dot_general(q, k, (((1,), (1,)), ((), ())), preferred_element_type=F32)
            sc = sc * scale + bias_ref[h]
            sc = jnp.where(valid, sc, NEG_INF)
            m = jnp.max(sc, axis=-1, keepdims=True)
            p = jnp.exp(sc - m)
            l = jnp.sum(p, axis=-1, keepdims=True)
            o = jnp.dot(p.astype(BF16), v, preferred_element_type=F32) / l
            o_ref[0, pl.ds(r0, sub), c0:c0 + HEAD_DIM] = o.astype(o_ref.dtype)
        return carry

    lax.fori_loop(0, qb // sub, sub_body, 0)


def _band_bias(rel_bias, sub):
    iq = np.arange(CHUNK)
    ik = np.arange(BAND)
    rel = np.clip(BAND_PAST + iq[:, None] - ik[None, :], REL_MIN, REL_MAX) - REL_MIN
    bias = rel_bias.astype(F32)[:, rel]
    nq = sub // CHUNK
    w = sub + BAND_PAST
    rows = []
    for m in range(nq):
        left = jnp.full((N_HEADS, CHUNK, m * CHUNK), NEG_INF, F32)
        right = jnp.full((N_HEADS, CHUNK, w - BAND - m * CHUNK), NEG_INF, F32)
        rows.append(jnp.concatenate([left, bias, right], axis=2))
    return jnp.concatenate(rows, axis=1)


def _band_attention(q_arr, q_col, kprev, kp_col, kcur, kc_col, vprev, vp_col, vcur, vc_col,
                    rel_bias, *, qb, sub, pos0, seq_len, prev_is_shifted):
    b, sq = q_arr.shape[0], q_arr.shape[1]
    bias = _band_bias(rel_bias, sub)
    w = sub + BAND_PAST
    if prev_is_shifted:
        assert qb == BAND_PAST
        prev_map = lambda col: (lambda bi, i: (bi, jnp.maximum(i - 1, 0), col))
    else:
        prev_map = lambda col: (lambda bi, i: (bi, 0, col))
    cur_map = lambda col: (lambda bi, i: (bi, i, col))
    kern = functools.partial(_attn_kernel, qb=qb, sub=sub, pos0=pos0, seq_len=seq_len)
    return pl.pallas_call(
        kern,
        grid=(b, sq // qb),
        in_specs=[pl.BlockSpec((1, qb, D_MODEL), cur_map(q_col)),
                  pl.BlockSpec((1, BAND_PAST, D_MODEL), prev_map(kp_col)),
                  pl.BlockSpec((1, qb, D_MODEL), cur_map(kc_col)),
                  pl.BlockSpec((1, BAND_PAST, D_MODEL), prev_map(vp_col)),
                  pl.BlockSpec((1, qb, D_MODEL), cur_map(vc_col)),
                  pl.BlockSpec((N_HEADS, sub, w), lambda bi, i: (0, 0, 0))],
        out_specs=pl.BlockSpec((1, qb, D_MODEL), lambda bi, i: (bi, i, 0)),
        out_shape=jax.ShapeDtypeStruct((b, sq, D_MODEL), BF16),
        scratch_shapes=[pltpu.VMEM((BAND_PAST + qb, D_MODEL), BF16),
                        pltpu.VMEM((BAND_PAST + qb, D_MODEL), BF16)],
        compiler_params=_cparams(("arbitrary", "arbitrary")),
        name="band_attn",
    )(q_arr, kprev, kcur, vprev, vcur, bias)


def _proj_ln_kernel(o_ref, w_ref, x_ref, g_ref, b_ref, y_ref, yb_ref):
    y = jnp.dot(o_ref[...], w_ref[...], preferred_element_type=F32)
    z = _layer_norm(ALPHA * x_ref[...] + y, g_ref[...], b_ref[...])
    y_ref[...] = z
    yb_ref[...] = z.astype(BF16)


def _proj_ln(o, w, x, g, b, tm):
    t = x.shape[0]
    tm = min(tm, t)
    row = lambda i: (i, 0)
    fixed = lambda i: (0, 0)
    return pl.pallas_call(
        _proj_ln_kernel,
        grid=(t // tm,),
        in_specs=[pl.BlockSpec((tm, D_MODEL), row),
                  pl.BlockSpec((D_MODEL, D_MODEL), fixed),
                  pl.BlockSpec((tm, D_MODEL), row),
                  pl.BlockSpec((1, D_MODEL), fixed),
                  pl.BlockSpec((1, D_MODEL), fixed)],
        out_specs=[pl.BlockSpec((tm, D_MODEL), row), pl.BlockSpec((tm, D_MODEL), row)],
        out_shape=[jax.ShapeDtypeStruct((t, D_MODEL), F32),
                   jax.ShapeDtypeStruct((t, D_MODEL), BF16)],
        compiler_params=_cparams(("arbitrary",)),
        name="proj_ln",
    )(o, w, x, g, b)


def _pool_kernel(x_ref, hist_ref, win_ref, wg_ref, sc_ref, g_ref, b_ref,
                 y_ref, yb_ref, st_ref, ext, *, tm, pos0):
    i = pl.program_id(1)

    @pl.when(i == 0)
    def _():
        ext[0:POOL_CARRY, :] = hist_ref[0]

    x = x_ref[0]
    u = jnp.dot(x.astype(BF16), win_ref[...], preferred_element_type=F32)
    ext[POOL_CARRY:POOL_CARRY + tm, :] = u
    row = lax.broadcasted_iota(jnp.int32, (tm, POOL_GROUP_DIM), 0)
    pos = (pos0 + i * tm + row + 1).astype(F32)
    zs = []
    for gi, wlen in enumerate(POOL_WINDOWS):
        c0 = gi * POOL_GROUP_DIM
        acc = u[:, c0:c0 + POOL_GROUP_DIM]
        for sft in range(1, wlen):
            acc = acc + ext[POOL_CARRY - sft:POOL_CARRY - sft + tm, c0:c0 + POOL_GROUP_DIM]
        mean = acc / jnp.minimum(float(wlen), pos)
        yg = (mean - u[:, c0:c0 + POOL_GROUP_DIM]).astype(BF16)
        zs.append(jnp.dot(yg, wg_ref[gi], preferred_element_type=F32))
    z = jnp.concatenate(zs, axis=1) * sc_ref[...]
    out = _layer_norm(ALPHA * x + z, g_ref[...], b_ref[...])
    y_ref[0] = out
    yb_ref[0] = out.astype(BF16)
    tail = ext[tm:tm + POOL_CARRY, :]
    st_ref[0] = tail
    ext[0:POOL_CARRY, :] = tail


def _pool_mixer_ln(x, hist, w_in, w_grp, scale, g, b, *, tm, pos0):
    bsz, s, _ = x.shape
    tm = min(tm, s)
    kern = functools.partial(_pool_kernel, tm=tm, pos0=pos0)
    blk = lambda bi, i: (bi, i, 0)
    per_b = lambda bi, i: (bi, 0, 0)
    fixed2 = lambda bi, i: (0, 0)
    return pl.pallas_call(
        kern,
        grid=(bsz, s // tm),
        in_specs=[pl.BlockSpec((1, tm, D_MODEL), blk),
                  pl.BlockSpec((1, POOL_CARRY, D_MODEL), per_b),
                  pl.BlockSpec((D_MODEL, D_MODEL), fixed2),
                  pl.BlockSpec((N_POOL_GROUPS, POOL_GROUP_DIM, POOL_GROUP_DIM),
                               lambda bi, i: (0, 0, 0)),
                  pl.BlockSpec((1, D_MODEL), fixed2),
                  pl.BlockSpec((1, D_MODEL), fixed2),
                  pl.BlockSpec((1, D_MODEL), fixed2)],
        out_specs=[pl.BlockSpec((1, tm, D_MODEL), blk),
                   pl.BlockSpec((1, tm, D_MODEL), blk),
                   pl.BlockSpec((1, POOL_CARRY, D_MODEL), per_b)],
        out_shape=[jax.ShapeDtypeStruct((bsz, s, D_MODEL), F32),
                   jax.ShapeDtypeStruct((bsz, s, D_MODEL), BF16),
                   jax.ShapeDtypeStruct((bsz, POOL_CARRY, D_MODEL), F32)],
        scratch_shapes=[pltpu.VMEM((POOL_CARRY + tm, D_MODEL), F32)],
        compiler_params=_cparams(("arbitrary", "arbitrary")),
        name="pool_mixer",
    )(x, hist, w_in, w_grp, scale, g, b)


def _extract_top(vals, n, dst_ref):
    for it in range(n):
        m = jnp.max(vals, axis=0, keepdims=True)
        dst_ref[it:it + 1, :] = m
        if it + 1 < n:
            vals = jnp.where(vals == m, -jnp.inf, vals)


def _route_kernel(x_ref, wq_ref, sk_ref, a_ref, b_ref, kap_ref, qt, r1, r2, cl, *, tb):
    qt[...] = lax.dot_general(wq_ref[...], x_ref[...], (((1,), (1,)), ((), ())),
                              preferred_element_type=F32)

    def head_body(h, carry):
        q1 = qt[pl.ds(pl.multiple_of(h * PEER_KEY_DIM, PEER_KEY_DIM), PEER_HALF), :]
        q2 = qt[pl.ds(pl.multiple_of(h * PEER_KEY_DIM + PEER_HALF, PEER_HALF), PEER_HALF), :]
        s1 = jnp.dot(sk_ref[h, 0], q1.astype(BF16), preferred_element_type=F32)
        s2 = jnp.dot(sk_ref[h, 1], q2.astype(BF16), preferred_element_type=F32)
        _extract_top(s1, PEER_TOPK, r1)
        _extract_top(s2, PEER_TOPK, r2)
        lo2 = r2[0:SUBLANES, :]
        tiles = [r1[a:a + 1, :] + lo2 for a in range(SUBLANES)]
        tiles.append(r1[0:1, :] + r2[SUBLANES:PEER_TOPK, :])
        tiles.append(r1[SUBLANES:PEER_TOPK, :] + r2[0:1, :])
        cand = jnp.concatenate(tiles, axis=0)
        _extract_top(cand, PEER_TOPK + 1, cl)
        c_top = cl[0:1, :]
        tau = 0.5 * (cl[PEER_TOPK - 1:PEER_TOPK, :] + cl[PEER_TOPK:PEER_TOPK + 1, :])
        zsum = jnp.sum(jnp.where(cand > tau, jnp.exp(cand - c_top), 0.0), axis=0, keepdims=True)
        inv_z = 1.0 / zsum
        a = jnp.where(s1 >= r1[PEER_TOPK - 1:PEER_TOPK, :], jnp.exp(s1 - r1[0:1, :]), 0.0)
        b = jnp.where(s2 >= r2[PEER_TOPK - 1:PEER_TOPK, :], jnp.exp(s2 - r2[0:1, :]), 0.0)
        a_ref[h] = a
        b_ref[h] = b * (0.5 * inv_z)
        kap_ref[h] = jnp.exp(tau - c_top) * (0.5 * inv_z)
        return carry

    lax.fori_loop(0, PEER_HEADS, head_body, 0)


def _peer_route(xb, wq_t, sk, tb):
    t = xb.shape[0]
    tb = min(tb, t)
    kern = functools.partial(_route_kernel, tb=tb)
    return pl.pallas_call(
        kern,
        grid=(t // tb,),
        in_specs=[pl.BlockSpec((tb, D_MODEL), lambda i: (i, 0)),
                  pl.BlockSpec((PEER_HEADS * PEER_KEY_DIM, D_MODEL), lambda i: (0, 0)),
                  pl.BlockSpec((PEER_HEADS, 2, N_KEYS, PEER_HALF), lambda i: (0, 0, 0, 0))],
        out_specs=[pl.BlockSpec((PEER_HEADS, N_KEYS, tb), lambda i: (0, 0, i)),
                   pl.BlockSpec((PEER_HEADS, N_KEYS, tb), lambda i: (0, 0, i)),
                   pl.BlockSpec((PEER_HEADS, 1, tb), lambda i: (0, 0, i))],
        out_shape=[jax.ShapeDtypeStruct((PEER_HEADS, N_KEYS, t), F32),
                   jax.ShapeDtypeStruct((PEER_HEADS, N_KEYS, t), F32),
                   jax.ShapeDtypeStruct((PEER_HEADS, 1, t), F32)],
        scratch_shapes=[pltpu.VMEM((PEER_HEADS * PEER_KEY_DIM, tb), F32),
                        pltpu.VMEM((PEER_TOPK, tb), F32),
                        pltpu.VMEM((PEER_TOPK, tb), F32),
                        pltpu.VMEM((3 * SUBLANES, tb), F32)],
        compiler_params=_cparams(("arbitrary",)),
        name="peer_route",
    )(xb, wq_t, sk)


def _peer_kernel(xb_ref, x_ref, u_ref, vt_ref, a_ref, b_ref, kap_ref, g_ref, bt_ref,
                 y_ref, acc, gt, *, tb, eb):
    e = pl.program_id(1)
    ni = eb // N_KEYS
    nlt = tb // LANES
    inv_sqrt2 = 2.0 ** -0.5

    @pl.when(e == 0)
    def _():
        acc[...] = jnp.zeros_like(acc)

    xb = xb_ref[...]
    for ii in range(ni):
        ht = lax.dot_general(u_ref[ii * N_KEYS:(ii + 1) * N_KEYS, :], xb,
                             (((1,), (1,)), ((), ())), preferred_element_type=F32)
        for lt in range(nlt):
            l0 = lt * LANES
            wsum = jnp.zeros((N_KEYS, LANES), F32)
            for h in range(PEER_HEADS):
                arow = a_ref[h, ii:ii + 1, l0:l0 + LANES]
                if ni < SUBLANES:
                    arow = jnp.where(e % 2 == 1, a_ref[h, ni + ii:ni + ii + 1, l0:l0 + LANES], arow)
                p = arow * b_ref[h, :, l0:l0 + LANES]
                wsum = wsum + jnp.where(p > kap_ref[h, :, l0:l0 + LANES], p, 0.0)
            hh = ht[:, l0:l0 + LANES]
            gval = wsum * (hh * (1.0 + lax.erf(hh * inv_sqrt2)))
            gt[ii * N_KEYS:(ii + 1) * N_KEYS, l0:l0 + LANES] = gval.astype(BF16)
    for mc in range(D_MODEL // PEER_ROW_CHUNK):
        r0 = mc * PEER_ROW_CHUNK
        acc[r0:r0 + PEER_ROW_CHUNK, :] += jnp.dot(vt_ref[r0:r0 + PEER_ROW_CHUNK, :], gt[...],
                                                  preferred_element_type=F32)

    @pl.when(e == pl.num_programs(1) - 1)
    def _():
        for lt in range(nlt):
            l0 = lt * LANES
            z = ALPHA * x_ref[l0:l0 + LANES, :] + acc[:, l0:l0 + LANES].T
            y_ref[l0:l0 + LANES, :] = _layer_norm(z, g_ref[...], bt_ref[...])


def _peer_dense_ln(xb, x, u, vt, a, b, kap, g, bt, *, tb, eb):
    t = x.shape[0]
    tb = min(tb, t)
    ni = eb // N_KEYS
    assert ni in (SUBLANES // 2, SUBLANES)
    kern = functools.partial(_peer_kernel, tb=tb, eb=eb)
    row = lambda i, e: (i, 0)
    fixed = lambda i, e: (0, 0)
    per_tok = lambda i, e: (0, 0, i)
    return pl.pallas_call(
        kern,
        grid=(t // tb, N_EXPERTS // eb),
        in_specs=[pl.BlockSpec((tb, D_MODEL), row),
                  pl.BlockSpec((tb, D_MODEL), row),
                  pl.BlockSpec((eb, D_MODEL), lambda i, e: (e, 0)),
                  pl.BlockSpec((D_MODEL, eb), lambda i, e: (0, e)),
                  pl.BlockSpec((PEER_HEADS, SUBLANES, tb), lambda i, e: (0, (e * ni) // SUBLANES, i)),
                  pl.BlockSpec((PEER_HEADS, N_KEYS, tb), per_tok),
                  pl.BlockSpec((PEER_HEADS, 1, tb), per_tok),
                  pl.BlockSpec((1, D_MODEL), fixed),
                  pl.BlockSpec((1, D_MODEL), fixed)],
        out_specs=pl.BlockSpec((tb, D_MODEL), row),
        out_shape=jax.ShapeDtypeStruct((t, D_MODEL), F32),
        scratch_shapes=[pltpu.VMEM((D_MODEL, tb), F32),
                        pltpu.VMEM((eb, tb), BF16)],
        compiler_params=_cparams(("arbitrary", "arbitrary")),
        name="peer_dense",
    )(xb, x, u, vt, a, b, kap, g, bt)


def _ple_kernel(x_ref, p_ref, wg_ref, wp_ref, y_ref):
    x = x_ref[...]
    gate = jax.nn.sigmoid(jnp.dot(x.astype(BF16), wg_ref[...], preferred_element_type=F32))
    emb = jnp.dot(p_ref[...].astype(BF16), wp_ref[...], preferred_element_type=F32)
    y_ref[...] = x + gate * emb


def _ple(x, p, wg, wp, tm):
    t = x.shape[0]
    tm = min(tm, t)
    row = lambda i: (i, 0)
    fixed = lambda i: (0, 0)
    return pl.pallas_call(
        _ple_kernel,
        grid=(t // tm,),
        in_specs=[pl.BlockSpec((tm, D_MODEL), row),
                  pl.BlockSpec((tm, PLE_DIM), row),
                  pl.BlockSpec((D_MODEL, D_MODEL), fixed),
                  pl.BlockSpec((PLE_DIM, D_MODEL), fixed)],
        out_specs=pl.BlockSpec((tm, D_MODEL), row),
        out_shape=jax.ShapeDtypeStruct((t, D_MODEL), F32),
        compiler_params=_cparams(("arbitrary",)),
        name="ple",
    )(x, p, wg, wp)


def _channel_mixer(x1, x1b, p, wts, *, tm, route_tb, peer_tb, peer_eb):
    a, b, kap = _peer_route(x1b, wts["wq_t"], wts["sk"], route_tb)
    x2 = _peer_dense_ln(x1b, x1, wts["u"], wts["vt"], a, b, kap, wts["g1"], wts["b1"],
                        tb=peer_tb, eb=peer_eb)
    return _ple(x2, p, wts["wg"], wts["wp"], tm)


def _attention_mixer(x, hist_k, hist_v, pos0, wts, *, tm, qb, sub):
    bsz, s, _ = x.shape
    xf = x.reshape(bsz * s, D_MODEL)
    qkv = _matmul(xf, wts["wqkv"], BF16, tm).reshape(bsz, s, 3 * D_MODEL)
    if hist_k is None:
        o = _band_attention(qkv, 0, qkv, 1, qkv, 1, qkv, 2, qkv, 2, wts["rel_bias"],
                            qb=qb, sub=sub, pos0=pos0, seq_len=s, prev_is_shifted=True)
    else:
        sp = -(-s // CHUNK) * CHUNK
        qkv_p = jnp.pad(qkv, ((0, 0), (0, sp - s), (0, 0)))
        o = _band_attention(qkv_p, 0, hist_k, 0, qkv_p, 1, hist_v, 0, qkv_p, 2, wts["rel_bias"],
                            qb=sp, sub=CHUNK, pos0=pos0, seq_len=s, prev_is_shifted=False)
        o = o[:, :s]
    x1, x1b = _proj_ln(o.reshape(bsz * s, D_MODEL), wts["wo"], xf, wts["g0"], wts["b0"], tm)
    return x1, x1b


def _new_kv(x_rows, wts, tm):
    return _matmul(x_rows, wts["wkv"], F32, tm)


def kernel(x_prompt, x_sample, cache_att_k, cache_att_v, state_pool, p_prompt, p_sample,
           att_w_qkv, att_w_o, att_rel_bias, pool_w_in, pool_w_grp, pool_scale,
           peer_w_query, peer_sub_keys, peer_u, peer_v, ln_g, ln_b, ple_w_proj, ple_w_gate):
    bp, sp_len, _ = x_prompt.shape
    bs, ss_len, _ = x_sample.shape
    tm = 512
    cfg_p = dict(tm=tm, route_tb=256, peer_tb=512, peer_eb=512)
    cfg_s = dict(tm=tm, route_tb=bs * ss_len, peer_tb=bs * ss_len, peer_eb=1024)
    xp, xs = x_prompt, x_sample
    kp_out, vp_out, ks_out, vs_out, pp_out, ps_out = [], [], [], [], [], []
    for i in range(DEPTH):
        j = i // 2
        wts = {
            "wq_t": peer_w_query[i].T.astype(BF16),
            "sk": peer_sub_keys[i].astype(BF16),
            "u": peer_u[i].astype(BF16),
            "vt": peer_v[i].astype(BF16).T,
            "g0": ln_g[i, 0].reshape(1, D_MODEL), "b0": ln_b[i, 0].reshape(1, D_MODEL),
            "g1": ln_g[i, 1].reshape(1, D_MODEL), "b1": ln_b[i, 1].reshape(1, D_MODEL),
            "wg": ple_w_gate[i].astype(BF16),
            "wp": ple_w_proj[i].astype(BF16),
        }
        if i % 2 == 0:
            wts["wqkv"] = att_w_qkv[j].astype(BF16)
            wts["wkv"] = wts["wqkv"][:, D_MODEL:]
            wts["wo"] = att_w_o[j].astype(BF16)
            wts["rel_bias"] = att_rel_bias[j]
            keep = min(BAND_PAST, sp_len)
            x1p, x1pb = _attention_mixer(xp, None, None, 0, wts, tm=tm, qb=BAND_PAST, sub=2 * CHUNK)
            kv_p = _new_kv(xp[:, sp_len - keep:].reshape(bp * keep, D_MODEL), wts, tm)
            kv_p = kv_p.reshape(bp, keep, 2, N_HEADS, HEAD_DIM)
            kp_out.append(kv_p[:, :, 0])
            vp_out.append(kv_p[:, :, 1])
            hk = cache_att_k[j].astype(BF16).reshape(bs, -1, D_MODEL)
            hv = cache_att_v[j].astype(BF16).reshape(bs, -1, D_MODEL)
            pad = BAND_PAST - hk.shape[1]
            hk = jnp.pad(hk, ((0, 0), (pad, 0), (0, 0)))
            hv = jnp.pad(hv, ((0, 0), (pad, 0), (0, 0)))
            x1s, x1sb = _attention_mixer(xs, hk, hv, PAST_LEN, wts, tm=tm, qb=CHUNK, sub=CHUNK)
            kv_s = _new_kv(xs.reshape(bs * ss_len, D_MODEL), wts, tm)
            kv_s = kv_s.reshape(bs, ss_len, 2, N_HEADS, HEAD_DIM)
            ks_out.append(kv_s[:, :, 0])
            vs_out.append(kv_s[:, :, 1])
        else:
            w_in = pool_w_in[j].astype(BF16)
            w_grp = pool_w_grp[j].astype(BF16)
            scale = pool_scale[j].reshape(1, D_MODEL)
            zero_hist = jnp.zeros((bp, POOL_CARRY, D_MODEL), F32)
            x1p, x1pb, st_p = _pool_mixer_ln(xp, zero_hist, w_in, w_grp, scale, wts["g0"], wts["b0"],
                                            tm=tm, pos0=0)
            hist_s = jnp.pad(state_pool[j], ((0, 0), (POOL_CARRY - POOL_HIST, 0), (0, 0)))
            x1s, x1sb, st_s = _pool_mixer_ln(xs, hist_s, w_in, w_grp, scale, wts["g0"], wts["b0"],
                                            tm=tm, pos0=PAST_LEN)
            x1p = x1p.reshape(bp * sp_len, D_MODEL)
            x1pb = x1pb.reshape(bp * sp_len, D_MODEL)
            x1s = x1s.reshape(bs * ss_len, D_MODEL)
            x1sb = x1sb.reshape(bs * ss_len, D_MODEL)
            pp_out.append(st_p[:, POOL_CARRY - POOL_HIST:])
            ps_out.append(st_s[:, POOL_CARRY - POOL_HIST:])
        xp = _channel_mixer(x1p, x1pb, p_prompt[i].reshape(bp * sp_len, PLE_DIM), wts,
                            **cfg_p).reshape(bp, sp_len, D_MODEL)
        xs = _channel_mixer(x1s, x1sb, p_sample[i].reshape(bs * ss_len, PLE_DIM), wts,
                            **cfg_s).reshape(bs, ss_len, D_MODEL)
    return (xp, xs, jnp.stack(kp_out), jnp.stack(vp_out), jnp.stack(ks_out), jnp.stack(vs_out),
            jnp.stack(pp_out), jnp.stack(ps_out))
```

```python
import functools

import jax
import jax.numpy as jnp
import numpy as np
from jax import lax
from jax.experimental import pallas as pl
from jax.experimental.pallas import tpu as pltpu

F32 = jnp.float32
BF16 = jnp.bfloat16

D_MODEL = 2048
DEPTH = 2
PAST_LEN = 4096
CHUNK = 64
N_HEADS = 16
HEAD_DIM = D_MODEL // N_HEADS
LEFT_CHUNKS = 8
BAND_PAST = LEFT_CHUNKS * CHUNK
BAND = BAND_PAST + CHUNK
REL_MAX = 256
REL_MIN = -(CHUNK - 1)
POOL_WINDOWS = (2, 4, 8, 16)
N_POOL_GROUPS = len(POOL_WINDOWS)
POOL_GROUP_DIM = D_MODEL // N_POOL_GROUPS
POOL_HIST = max(POOL_WINDOWS) - 1
POOL_CARRY = 16
PEER_HEADS = 8
PEER_KEY_DIM = 256
PEER_HALF = PEER_KEY_DIM // 2
N_KEYS = 128
N_EXPERTS = N_KEYS * N_KEYS
PEER_TOPK = 16
PLE_DIM = 256
PEER_ROW_CHUNK = 512
ALPHA = (2 * DEPTH) ** 0.25
LN_EPS = 1e-5
NEG_INF = -1e30

LANES = 128
SUBLANES = 8
VMEM_LIMIT_BYTES = 56 * 1024 * 1024


def _cparams(sem):
    return pltpu.CompilerParams(dimension_semantics=sem, vmem_limit_bytes=VMEM_LIMIT_BYTES)


def _layer_norm(z, g, b):
    mu = jnp.mean(z, axis=-1, keepdims=True)
    zc = z - mu
    var = jnp.mean(zc * zc, axis=-1, keepdims=True)
    return zc * lax.rsqrt(var + LN_EPS) * g + b


def _mm_kernel(x_ref, w_ref, o_ref):
    x = x_ref[...].astype(BF16)
    o_ref[...] = jnp.dot(x, w_ref[...], preferred_element_type=F32).astype(o_ref.dtype)


def _matmul(x, w, out_dtype, tm):
    t, k = x.shape
    n = w.shape[1]
    tn = min(n, D_MODEL)
    tm = min(tm, t)
    return pl.pallas_call(
        _mm_kernel,
        grid=(n // tn, t // tm),
        in_specs=[pl.BlockSpec((tm, k), lambda j, i: (i, 0)),
                  pl.BlockSpec((k, tn), lambda j, i: (0, j))],
        out_specs=pl.BlockSpec((tm, tn), lambda j, i: (i, j)),
        out_shape=jax.ShapeDtypeStruct((t, n), out_dtype),
        compiler_params=_cparams(("arbitrary", "arbitrary")),
        name="proj",
    )(x, w)


def _attn_kernel(q_ref, kp_ref, kc_ref, vp_ref, vc_ref, bias_ref, o_ref, kwin, vwin,
                 *, qb, sub, pos0, seq_len):
    i = pl.program_id(1)
    w = sub + BAND_PAST
    kwin[0:BAND_PAST, :] = kp_ref[0]
    kwin[BAND_PAST:BAND_PAST + qb, :] = kc_ref[0]
    vwin[0:BAND_PAST, :] = vp_ref[0]
    vwin[BAND_PAST:BAND_PAST + qb, :] = vc_ref[0]
    scale = HEAD_DIM ** -0.5
    col = lax.broadcasted_iota(jnp.int32, (sub, w), 1)

    def sub_body(s, carry):
        r0 = pl.multiple_of(s * sub, sub)
        kpos = (pos0 - BAND_PAST) + i * qb + r0 + col
        valid = (kpos >= 0) & (kpos < pos0 + seq_len)
        for h in range(N_HEADS):
            c0 = h * HEAD_DIM
            q = q_ref[0, pl.ds(r0, sub), c0:c0 + HEAD_DIM]
            k = kwin[pl.ds(r0, w), c0:c0 + HEAD_DIM]
            v = vwin[pl.ds(r0, w), c0:c0 + HEAD_DIM]
            sc = lax.dot_general(q, k, (((1,), (1,)), ((), ())), preferred_element_type=F32)
            sc = sc * scale + bias_ref[h]
            sc = jnp.where(valid, sc, NEG_INF)
            m = jnp.max(sc, axis=-1, keepdims=True)
            p = jnp.exp(sc - m)
            l = jnp.sum(p, axis=-1, keepdims=True)
            o = jnp.dot(p.astype(BF16), v, preferred_element_type=F32) / l
            o_ref[0, pl.ds(r0, sub), c0:c0 + HEAD_DIM] = o.astype(o_ref.dtype)
        return carry

    lax.fori_loop(0, qb // sub, sub_body, 0)


def _band_bias(rel_bias, sub):
    iq = np.arange(CHUNK)
    ik = np.arange(BAND)
    rel = np.clip(BAND_PAST + iq[:, None] - ik[None, :], REL_MIN, REL_MAX) - REL_MIN
    bias = rel_bias.astype(F32)[:, rel]
    nq = sub // CHUNK
    w = sub + BAND_PAST
    rows = []
    for m in range(nq):
        left = jnp.full((N_HEADS, CHUNK, m * CHUNK), NEG_INF, F32)
        right = jnp.full((N_HEADS, CHUNK, w - BAND - m * CHUNK), NEG_INF, F32)
        rows.append(jnp.concatenate([left, bias, right], axis=2))
    return jnp.concatenate(rows, axis=1)


def _band_attention(q_arr, q_col, kprev, kp_col, kcur, kc_col, vprev, vp_col, vcur, vc_col,
                    rel_bias, *, qb, sub, pos0, seq_len, prev_is_shifted):
    b, sq = q_arr.shape[0], q_arr.shape[1]
    bias = _band_bias(rel_bias, sub)
    w = sub + BAND_PAST
    if prev_is_shifted:
        assert qb == BAND_PAST
        prev_map = lambda col: (lambda bi, i: (bi, jnp.maximum(i - 1, 0), col))
    else:
        prev_map = lambda col: (lambda bi, i: (bi, 0, col))
    cur_map = lambda col: (lambda bi, i: (bi, i, col))
    kern = functools.partial(_attn_kernel, qb=qb, sub=sub, pos0=pos0, seq_len=seq_len)
    return pl.pallas_call(
        kern,
        grid=(b, sq // qb),
        in_specs=[pl.BlockSpec((1, qb, D_MODEL), cur_map(q_col)),
                  pl.BlockSpec((1, BAND_PAST, D_MODEL), prev_map(kp_col)),
                  pl.BlockSpec((1, qb, D_MODEL), cur_map(kc_col)),
                  pl.BlockSpec((1, BAND_PAST, D_MODEL), prev_map(vp_col)),
                  pl.BlockSpec((1, qb, D_MODEL), cur_map(vc_col)),
                  pl.BlockSpec((N_HEADS, sub, w), lambda bi, i: (0, 0, 0))],
        out_specs=pl.BlockSpec((1, qb, D_MODEL), lambda bi, i: (bi, i, 0)),
        out_shape=jax.ShapeDtypeStruct((b, sq, D_MODEL), BF16),
        scratch_shapes=[pltpu.VMEM((BAND_PAST + qb, D_MODEL), BF16),
                        pltpu.VMEM((BAND_PAST + qb, D_MODEL), BF16)],
        compiler_params=_cparams(("arbitrary", "arbitrary")),
        name="band_attn",
    )(q_arr, kprev, kcur, vprev, vcur, bias)


def _proj_ln_kernel(o_ref, w_ref, x_ref, g_ref, b_ref, y_ref, yb_ref):
    y = jnp.dot(o_ref[...], w_ref[...], preferred_element_type=F32)
    z = _layer_norm(ALPHA * x_ref[...] + y, g_ref[...], b_ref[...])
    y_ref[...] = z
    yb_ref[...] = z.astype(BF16)


def _proj_ln(o, w, x, g, b, tm):
    t = x.shape[0]
    tm = min(tm, t)
    row = lambda i: (i, 0)
    fixed = lambda i: (0, 0)
    return pl.pallas_call(
        _proj_ln_kernel,
        grid=(t // tm,),
        in_specs=[pl.BlockSpec((tm, D_MODEL), row),
                  pl.BlockSpec((D_MODEL, D_MODEL), fixed),
                  pl.BlockSpec((tm, D_MODEL), row),
                  pl.BlockSpec((1, D_MODEL), fixed),
                  pl.BlockSpec((1, D_MODEL), fixed)],
        out_specs=[pl.BlockSpec((tm, D_MODEL), row), pl.BlockSpec((tm, D_MODEL), row)],
        out_shape=[jax.ShapeDtypeStruct((t, D_MODEL), F32),
                   jax.ShapeDtypeStruct((t, D_MODEL), BF16)],
        compiler_params=_cparams(("arbitrary",)),
        name="proj_ln",
    )(o, w, x, g, b)


def _pool_kernel(x_ref, hist_ref, win_ref, wg_ref, sc_ref, g_ref, b_ref,
                 y_ref, yb_ref, st_ref, ext, *, tm, pos0):
    i = pl.program_id(1)

    @pl.when(i == 0)
    def _():
        ext[0:POOL_CARRY, :] = hist_ref[0]

    x = x_ref[0]
    u = jnp.dot(x.astype(BF16), win_ref[...], preferred_element_type=F32)
    ext[POOL_CARRY:POOL_CARRY + tm, :] = u
    row = lax.broadcasted_iota(jnp.int32, (tm, POOL_GROUP_DIM), 0)
    pos = (pos0 + i * tm + row + 1).astype(F32)
    zs = []
    for gi, wlen in enumerate(POOL_WINDOWS):
        c0 = gi * POOL_GROUP_DIM
        acc = u[:, c0:c0 + POOL_GROUP_DIM]
        for sft in range(1, wlen):
            acc = acc + ext[POOL_CARRY - sft:POOL_CARRY - sft + tm, c0:c0 + POOL_GROUP_DIM]
        mean = acc / jnp.minimum(float(wlen), pos)
        yg = (mean - u[:, c0:c0 + POOL_GROUP_DIM]).astype(BF16)
        zs.append(jnp.dot(yg, wg_ref[gi], preferred_element_type=F32))
    z = jnp.concatenate(zs, axis=1) * sc_ref[...]
    out = _layer_norm(ALPHA * x + z, g_ref[...], b_ref[...])
    y_ref[0] = out
    yb_ref[0] = out.astype(BF16)
    tail = ext[tm:tm + POOL_CARRY, :]
    st_ref[0] = tail
    ext[0:POOL_CARRY, :] = tail


def _pool_mixer_ln(x, hist, w_in, w_grp, scale, g, b, *, tm, pos0):
    bsz, s, _ = x.shape
    tm = min(tm, s)
    kern = functools.partial(_pool_kernel, tm=tm, pos0=pos0)
    blk = lambda bi, i: (bi, i, 0)
    per_b = lambda bi, i: (bi, 0, 0)
    fixed2 = lambda bi, i: (0, 0)
    return pl.pallas_call(
        kern,
        grid=(bsz, s // tm),
        in_specs=[pl.BlockSpec((1, tm, D_MODEL), blk),
                  pl.BlockSpec((1, POOL_CARRY, D_MODEL), per_b),
                  pl.BlockSpec((D_MODEL, D_MODEL), fixed2),
                  pl.BlockSpec((N_POOL_GROUPS, POOL_GROUP_DIM, POOL_GROUP_DIM),
                               lambda bi, i: (0, 0, 0)),
                  pl.BlockSpec((1, D_MODEL), fixed2),
                  pl.BlockSpec((1, D_MODEL), fixed2),
                  pl.BlockSpec((1, D_MODEL), fixed2)],
        out_specs=[pl.BlockSpec((1, tm, D_MODEL), blk),
                   pl.BlockSpec((1, tm, D_MODEL), blk),
                   pl.BlockSpec((1, POOL_CARRY, D_MODEL), per_b)],
        out_shape=[jax.ShapeDtypeStruct((bsz, s, D_MODEL), F32),
                   jax.ShapeDtypeStruct((bsz, s, D_MODEL), BF16),
                   jax.ShapeDtypeStruct((bsz, POOL_CARRY, D_MODEL), F32)],
        scratch_shapes=[pltpu.VMEM((POOL_CARRY + tm, D_MODEL), F32)],
        compiler_params=_cparams(("arbitrary", "arbitrary")),
        name="pool_mixer",
    )(x, hist, w_in, w_grp, scale, g, b)


def _extract_top(vals, n, dst_ref):
    for it in range(n):
        m = jnp.max(vals, axis=0, keepdims=True)
        dst_ref[it:it + 1, :] = m
        if it + 1 < n:
            vals = jnp.where(vals == m, -jnp.inf, vals)


def _route_kernel(x_ref, wq_ref, sk_ref, a_ref, b_ref, kap_ref, qt, r1, r2, cl, *, tb):
    qt[...] = lax.dot_general(wq_ref[...], x_ref[...], (((1,), (1,)), ((), ())),
                              preferred_element_type=F32)

    def head_body(h, carry):
        q1 = qt[pl.ds(pl.multiple_of(h * PEER_KEY_DIM, PEER_KEY_DIM), PEER_HALF), :]
        q2 = qt[pl.ds(pl.multiple_of(h * PEER_KEY_DIM + PEER_HALF, PEER_HALF), PEER_HALF), :]
        s1 = jnp.dot(sk_ref[h, 0], q1.astype(BF16), preferred_element_type=F32)
        s2 = jnp.dot(sk_ref[h, 1], q2.astype(BF16), preferred_element_type=F32)
        _extract_top(s1, PEER_TOPK, r1)
        _extract_top(s2, PEER_TOPK, r2)
        lo2 = r2[0:SUBLANES, :]
        tiles = [r1[a:a + 1, :] + lo2 for a in range(SUBLANES)]
        tiles.append(r1[0:1, :] + r2[SUBLANES:PEER_TOPK, :])
        tiles.append(r1[SUBLANES:PEER_TOPK, :] + r2[0:1, :])
        cand = jnp.concatenate(tiles, axis=0)
        _extract_top(cand, PEER_TOPK + 1, cl)
        c_top = cl[0:1, :]
        tau = 0.5 * (cl[PEER_TOPK - 1:PEER_TOPK, :] + cl[PEER_TOPK:PEER_TOPK + 1, :])
        zsum = jnp.sum(jnp.where(cand > tau, jnp.exp(cand - c_top), 0.0), axis=0, keepdims=True)
        inv_z = 1.0 / zsum
        a = jnp.where(s1 >= r1[PEER_TOPK - 1:PEER_TOPK, :], jnp.exp(s1 - r1[0:1, :]), 0.0)
        b = jnp.where(s2 >= r2[PEER_TOPK - 1:PEER_TOPK, :], jnp.exp(s2 - r2[0:1, :]), 0.0)
        a_ref[h] = a
        b_ref[h] = b * (0.5 * inv_z)
        kap_ref[h] = jnp.exp(tau - c_top) * (0.5 * inv_z)
        return carry

    lax.fori_loop(0, PEER_HEADS, head_body, 0)


def _peer_route(xb, wq_t, sk, tb):
    t = xb.shape[0]
    tb = min(tb, t)
    kern = functools.partial(_route_kernel, tb=tb)
    return pl.pallas_call(
        kern,
        grid=(t // tb,),
        in_specs=[pl.BlockSpec((tb, D_MODEL), lambda i: (i, 0)),
                  pl.BlockSpec((PEER_HEADS * PEER_KEY_DIM, D_MODEL), lambda i: (0, 0)),
                  pl.BlockSpec((PEER_HEADS, 2, N_KEYS, PEER_HALF), lambda i: (0, 0, 0, 0))],
        out_specs=[pl.BlockSpec((PEER_HEADS, N_KEYS, tb), lambda i: (0, 0, i)),
                   pl.BlockSpec((PEER_HEADS, N_KEYS, tb), lambda i: (0, 0, i)),
                   pl.BlockSpec((PEER_HEADS, 1, tb), lambda i: (0, 0, i))],
        out_shape=[jax.ShapeDtypeStruct((PEER_HEADS, N_KEYS, t), F32),
                   jax.ShapeDtypeStruct((PEER_HEADS, N_KEYS, t), F32),
                   jax.ShapeDtypeStruct((PEER_HEADS, 1, t), F32)],
        scratch_shapes=[pltpu.VMEM((PEER_HEADS * PEER_KEY_DIM, tb), F32),
                        pltpu.VMEM((PEER_TOPK, tb), F32),
                        pltpu.VMEM((PEER_TOPK, tb), F32),
                        pltpu.VMEM((3 * SUBLANES, tb), F32)],
        compiler_params=_cparams(("arbitrary",)),
        name="peer_route",
    )(xb, wq_t, sk)


def _peer_kernel(xb_ref, x_ref, u_ref, vt_ref, a_ref, b_ref, kap_ref,
                 g_ref, bt_ref, y_ref, acc, ht, gt, *, tb, eb):
    e = pl.program_id(1)
    ni = eb // N_KEYS
    nlt = tb // LANES
    inv_sqrt2 = 2.0 ** -0.5

    @pl.when(e == 0)
    def _():
        acc[...] = jnp.zeros_like(acc)

    xb = xb_ref[...]

    def stage_a(half, slot):
        ht[slot] = lax.dot_general(u_ref[half * eb:(half + 1) * eb, :], xb,
                                   (((1,), (1,)), ((), ())), preferred_element_type=F32)

    def stage_b(slot, row0):
        for ii in range(ni):
            for lt in range(nlt):
                l0 = lt * LANES
                wsum = jnp.zeros((N_KEYS, LANES), F32)
                for h in range(PEER_HEADS):
                    p = a_ref[h, row0 + ii:row0 + ii + 1, l0:l0 + LANES] * b_ref[h, :, l0:l0 + LANES]
                    wsum = wsum + jnp.where(p > kap_ref[h, :, l0:l0 + LANES], p, 0.0)
                hh = ht[slot, ii * N_KEYS:(ii + 1) * N_KEYS, l0:l0 + LANES]
                gval = wsum * (hh * (1.0 + lax.erf(hh * inv_sqrt2)))
                gt[slot, ii * N_KEYS:(ii + 1) * N_KEYS, l0:l0 + LANES] = gval.astype(BF16)

    def stage_c(slot, half):
        for mc in range(D_MODEL // PEER_ROW_CHUNK):
            r0 = mc * PEER_ROW_CHUNK
            acc[r0:r0 + PEER_ROW_CHUNK, :] += jnp.dot(
                vt_ref[r0:r0 + PEER_ROW_CHUNK, half * eb:(half + 1) * eb], gt[slot],
                preferred_element_type=F32)

    stage_a(0, 0)
    stage_a(1, 1)
    stage_b(0, 0)
    stage_c(0, 0)
    stage_b(1, ni)
    stage_c(1, 1)

    @pl.when(e == pl.num_programs(1) - 1)
    def _():
        for lt in range(nlt):
            l0 = lt * LANES
            z = ALPHA * x_ref[l0:l0 + LANES, :] + acc[:, l0:l0 + LANES].T
            y_ref[l0:l0 + LANES, :] = _layer_norm(z, g_ref[...], bt_ref[...])


def _peer_dense_ln(xb, x, u, vt, a, b, kap, g, bt, *, tb, eb):
    t = x.shape[0]
    tb = min(tb, t)
    assert 2 * eb == SUBLANES * N_KEYS
    n_e = N_EXPERTS // (2 * eb)
    kern = functools.partial(_peer_kernel, tb=tb, eb=eb)
    row = lambda i, e: (i, 0)
    fixed = lambda i, e: (0, 0)
    per_tok = lambda i, e: (0, 0, i)
    return pl.pallas_call(
        kern,
        grid=(t // tb, n_e),
        in_specs=[pl.BlockSpec((tb, D_MODEL), row),
                  pl.BlockSpec((tb, D_MODEL), row),
                  pl.BlockSpec((2 * eb, D_MODEL), lambda i, e: (e, 0)),
                  pl.BlockSpec((D_MODEL, 2 * eb), lambda i, e: (0, e)),
                  pl.BlockSpec((PEER_HEADS, SUBLANES, tb), lambda i, e: (0, e, i)),
                  pl.BlockSpec((PEER_HEADS, N_KEYS, tb), per_tok),
                  pl.BlockSpec((PEER_HEADS, 1, tb), per_tok),
                  pl.BlockSpec((1, D_MODEL), fixed),
                  pl.BlockSpec((1, D_MODEL), fixed)],
        out_specs=pl.BlockSpec((tb, D_MODEL), row),
        out_shape=jax.ShapeDtypeStruct((t, D_MODEL), F32),
        scratch_shapes=[pltpu.VMEM((D_MODEL, tb), F32),
                        pltpu.VMEM((2, eb, tb), F32),
                        pltpu.VMEM((2, eb, tb), BF16)],
        compiler_params=_cparams(("arbitrary", "arbitrary")),
        name="peer_dense",
    )(xb, x, u, vt, a, b, kap, g, bt)


def _ple_kernel(x_ref, p_ref, wg_ref, wp_ref, y_ref):
    x = x_ref[...]
    gate = jax.nn.sigmoid(jnp.dot(x.astype(BF16), wg_ref[...], preferred_element_type=F32))
    emb = jnp.dot(p_ref[...].astype(BF16), wp_ref[...], preferred_element_type=F32)
    y_ref[...] = x + gate * emb


def _ple(x, p, wg, wp, tm):
    t = x.shape[0]
    tm = min(tm, t)
    row = lambda i: (i, 0)
    fixed = lambda i: (0, 0)
    return pl.pallas_call(
        _ple_kernel,
        grid=(t // tm,),
        in_specs=[pl.BlockSpec((tm, D_MODEL), row),
                  pl.BlockSpec((tm, PLE_DIM), row),
                  pl.BlockSpec((D_MODEL, D_MODEL), fixed),
                  pl.BlockSpec((PLE_DIM, D_MODEL), fixed)],
        out_specs=pl.BlockSpec((tm, D_MODEL), row),
        out_shape=jax.ShapeDtypeStruct((t, D_MODEL), F32),
        compiler_params=_cparams(("arbitrary",)),
        name="ple",
    )(x, p, wg, wp)


def _channel_mixer(x1, x1b, p, wts, *, tm, route_tb, peer_tb, peer_eb):
    a, b, kap = _peer_route(x1b, wts["wq_t"], wts["sk"], route_tb)
    x2 = _peer_dense_ln(x1b, x1, wts["u"], wts["vt"], a, b, kap, wts["g1"], wts["b1"],
                        tb=peer_tb, eb=peer_eb)
    return _ple(x2, p, wts["wg"], wts["wp"], tm)


def _attention_mixer(x, hist_k, hist_v, pos0, wts, *, tm, qb, sub):
    bsz, s, _ = x.shape
    xf = x.reshape(bsz * s, D_MODEL)
    qkv = _matmul(xf, wts["wqkv"], BF16, tm).reshape(bsz, s, 3 * D_MODEL)
    if hist_k is None:
        o = _band_attention(qkv, 0, qkv, 1, qkv, 1, qkv, 2, qkv, 2, wts["rel_bias"],
                            qb=qb, sub=sub, pos0=pos0, seq_len=s, prev_is_shifted=True)
    else:
        sp = -(-s // CHUNK) * CHUNK
        qkv_p = jnp.pad(qkv, ((0, 0), (0, sp - s), (0, 0)))
        o = _band_attention(qkv_p, 0, hist_k, 0, qkv_p, 1, hist_v, 0, qkv_p, 2, wts["rel_bias"],
                            qb=sp, sub=CHUNK, pos0=pos0, seq_len=s, prev_is_shifted=False)
        o = o[:, :s]
    x1, x1b = _proj_ln(o.reshape(bsz * s, D_MODEL), wts["wo"], xf, wts["g0"], wts["b0"], tm)
    return x1, x1b


def _new_kv(x_rows, wts, tm):
    return _matmul(x_rows, wts["wkv"], F32, tm)


def kernel(x_prompt, x_sample, cache_att_k, cache_att_v, state_pool, p_prompt, p_sample,
           att_w_qkv, att_w_o, att_rel_bias, pool_w_in, pool_w_grp, pool_scale,
           peer_w_query, peer_sub_keys, peer_u, peer_v, ln_g, ln_b, ple_w_proj, ple_w_gate):
    bp, sp_len, _ = x_prompt.shape
    bs, ss_len, _ = x_sample.shape
    tm = 512
    cfg_p = dict(tm=tm, route_tb=256, peer_tb=512, peer_eb=512)
    cfg_s = dict(tm=tm, route_tb=bs * ss_len, peer_tb=bs * ss_len, peer_eb=512)
    xp, xs = x_prompt, x_sample
    kp_out, vp_out, ks_out, vs_out, pp_out, ps_out = [], [], [], [], [], []
    for i in range(DEPTH):
        j = i // 2
        wts = {
            "wq_t": peer_w_query[i].T.astype(BF16),
            "sk": peer_sub_keys[i].astype(BF16),
            "u": peer_u[i].astype(BF16),
            "vt": peer_v[i].astype(BF16).T,
            "g0": ln_g[i, 0].reshape(1, D_MODEL), "b0": ln_b[i, 0].reshape(1, D_MODEL),
            "g1": ln_g[i, 1].reshape(1, D_MODEL), "b1": ln_b[i, 1].reshape(1, D_MODEL),
            "wg": ple_w_gate[i].astype(BF16),
            "wp": ple_w_proj[i].astype(BF16),
        }
        if i % 2 == 0:
            wts["wqkv"] = att_w_qkv[j].astype(BF16)
            wts["wkv"] = wts["wqkv"][:, D_MODEL:]
            wts["wo"] = att_w_o[j].astype(BF16)
            wts["rel_bias"] = att_rel_bias[j]
            keep = min(BAND_PAST, sp_len)
            x1p, x1pb = _attention_mixer(xp, None, None, 0, wts, tm=tm, qb=BAND_PAST, sub=2 * CHUNK)
            kv_p = _new_kv(xp[:, sp_len - keep:].reshape(bp * keep, D_MODEL), wts, tm)
            kv_p = kv_p.reshape(bp, keep, 2, N_HEADS, HEAD_DIM)
            kp_out.append(kv_p[:, :, 0])
            vp_out.append(kv_p[:, :, 1])
            hk = cache_att_k[j].astype(BF16).reshape(bs, -1, D_MODEL)
            hv = cache_att_v[j].astype(BF16).reshape(bs, -1, D_MODEL)
            pad = BAND_PAST - hk.shape[1]
            hk = jnp.pad(hk, ((0, 0), (pad, 0), (0, 0)))
            hv = jnp.pad(hv, ((0, 0), (pad, 0), (0, 0)))
            x1s, x1sb = _attention_mixer(xs, hk, hv, PAST_LEN, wts, tm=tm, qb=CHUNK, sub=CHUNK)
            kv_s = _new_kv(xs.reshape(bs * ss_len, D_MODEL), wts, tm)
            kv_s = kv_s.reshape(bs, ss_len, 2, N_HEADS, HEAD_DIM)
            ks_out.append(kv_s[:, :, 0])
            vs_out.append(kv_s[:, :, 1])
        else:
            w_in = pool_w_in[j].astype(BF16)
            w_grp = pool_w_grp[j].astype(BF16)
            scale = pool_scale[j].reshape(1, D_MODEL)
            zero_hist = jnp.zeros((bp, POOL_CARRY, D_MODEL), F32)
            x1p, x1pb, st_p = _pool_mixer_ln(xp, zero_hist, w_in, w_grp, scale, wts["g0"], wts["b0"],
                                            tm=tm, pos0=0)
            hist_s = jnp.pad(state_pool[j], ((0, 0), (POOL_CARRY - POOL_HIST, 0), (0, 0)))
            x1s, x1sb, st_s = _pool_mixer_ln(xs, hist_s, w_in, w_grp, scale, wts["g0"], wts["b0"],
                                            tm=tm, pos0=PAST_LEN)
            x1p = x1p.reshape(bp * sp_len, D_MODEL)
            x1pb = x1pb.reshape(bp * sp_len, D_MODEL)
            x1s = x1s.reshape(bs * ss_len, D_MODEL)
            x1sb = x1sb.reshape(bs * ss_len, D_MODEL)
            pp_out.append(st_p[:, POOL_CARRY - POOL_HIST:])
            ps_out.append(st_s[:, POOL_CARRY - POOL_HIST:])
        xp = _channel_mixer(x1p, x1pb, p_prompt[i].reshape(bp * sp_len, PLE_DIM), wts,
                            **cfg_p).reshape(bp, sp_len, D_MODEL)
        xs = _channel_mixer(x1s, x1sb, p_sample[i].reshape(bs * ss_len, PLE_DIM), wts,
                            **cfg_s).reshape(bs, ss_len, D_MODEL)
    return (xp, xs, jnp.stack(kp_out), jnp.stack(vp_out), jnp.stack(ks_out), jnp.stack(vs_out),
            jnp.stack(pp_out), jnp.stack(ps_out))
```

```python
import functools

import jax
import jax.numpy as jnp
import numpy as np
from jax import lax
from jax.experimental import pallas as pl
from jax.experimental.pallas import tpu as pltpu

F32 = jnp.float32
BF16 = jnp.bfloat16

D_MODEL = 2048
DEPTH = 2
PAST_LEN = 4096
CHUNK = 64
N_HEADS = 16
HEAD_DIM = D_MODEL // N_HEADS
LEFT_CHUNKS = 8
BAND_PAST = LEFT_CHUNKS * CHUNK
BAND = BAND_PAST + CHUNK
REL_MAX = 256
REL_MIN = -(CHUNK - 1)
POOL_WINDOWS = (2, 4, 8, 16)
N_POOL_GROUPS = len(POOL_WINDOWS)
POOL_GROUP_DIM = D_MODEL // N_POOL_GROUPS
POOL_HIST = max(POOL_WINDOWS) - 1
POOL_CARRY = 16
PEER_HEADS = 8
PEER_KEY_DIM = 256
PEER_HALF = PEER_KEY_DIM // 2
N_KEYS = 128
N_EXPERTS = N_KEYS * N_KEYS
PEER_TOPK = 16
PLE_DIM = 256
PEER_ROW_CHUNK = 512
ROUTE_HEAD_UNROLL = 4
ALPHA = (2 * DEPTH) ** 0.25
LN_EPS = 1e-5
NEG_INF = -1e30

LANES = 128
SUBLANES = 8
MXU_COLS = 256
VMEM_LIMIT_BYTES = 56 * 1024 * 1024


def _cparams(sem):
    return pltpu.CompilerParams(dimension_semantics=sem, vmem_limit_bytes=VMEM_LIMIT_BYTES)


def _layer_norm(z, g, b):
    mu = jnp.mean(z, axis=-1, keepdims=True)
    zc = z - mu
    var = jnp.mean(zc * zc, axis=-1, keepdims=True)
    return zc * lax.rsqrt(var + LN_EPS) * g + b


def _mm_kernel(x_ref, w_ref, o_ref):
    x = x_ref[...].astype(BF16)
    o_ref[...] = jnp.dot(x, w_ref[...], preferred_element_type=F32).astype(o_ref.dtype)


def _matmul(x, w, out_dtype, tm):
    t, k = x.shape
    n = w.shape[1]
    tn = min(n, D_MODEL)
    tm = min(tm, t)
    return pl.pallas_call(
        _mm_kernel,
        grid=(n // tn, t // tm),
        in_specs=[pl.BlockSpec((tm, k), lambda j, i: (i, 0)),
                  pl.BlockSpec((k, tn), lambda j, i: (0, j))],
        out_specs=pl.BlockSpec((tm, tn), lambda j, i: (i, j)),
        out_shape=jax.ShapeDtypeStruct((t, n), out_dtype),
        compiler_params=_cparams(("arbitrary", "arbitrary")),
        name="proj",
    )(x, w)


def _attn_kernel(q_ref, kp_ref, kc_ref, vp_ref, vc_ref, bias_ref, o_ref, kwin, vwin,
                 *, qb, sub, pos0, seq_len):
    i = pl.program_id(1)
    w = sub + BAND_PAST
    kwin[0:BAND_PAST, :] = kp_ref[0]
    kwin[BAND_PAST:BAND_PAST + qb, :] = kc_ref[0]
    vwin[0:BAND_PAST, :] = vp_ref[0]
    vwin[BAND_PAST:BAND_PAST + qb, :] = vc_ref[0]
    scale = HEAD_DIM ** -0.5
    col = lax.broadcasted_iota(jnp.int32, (sub, w), 1)

    def sub_body(s, carry):
        r0 = pl.multiple_of(s * sub, sub)
        kpos = (pos0 - BAND_PAST) + i * qb + r0 + col
        valid = (kpos >= 0) & (kpos < pos0 + seq_len)
        for h in range(N_HEADS):
            c0 = h * HEAD_DIM
            q = q_ref[0, pl.ds(r0, sub), c0:c0 + HEAD_DIM]
            k = kwin[pl.ds(r0, w), c0:c0 + HEAD_DIM]
            v = vwin[pl.ds(r0, w), c0:c0 + HEAD_DIM]
            sc = lax.dot_general(q, k, (((1,), (1,)), ((), ())), preferred_element_type=F32)
            sc = sc * scale + bias_ref[h]
            sc = jnp.where(valid, sc, NEG_INF)
            m = jnp.max(sc, axis=-1, keepdims=True)
            p = jnp.exp(sc - m)
            l = jnp.sum(p, axis=-1, keepdims=True)
            o = jnp.dot(p.astype(BF16), v, preferred_element_type=F32) / l
            o_ref[0, pl.ds(r0, sub), c0:c0 + HEAD_DIM] = o.astype(o_ref.dtype)
        return carry

    lax.fori_loop(0, qb // sub, sub_body, 0)


def _band_bias(rel_bias, sub):
    n_rel = REL_MAX - REL_MIN + 1
    rev = rel_bias.astype(F32)[:, ::-1]
    lead = BAND_PAST - REL_MAX - REL_MIN
    ext = jnp.concatenate([jnp.broadcast_to(rev[:, :1], (N_HEADS, lead)), rev], axis=1)
    assert ext.shape[1] == CHUNK - 1 + BAND and lead + n_rel == ext.shape[1]
    bias = jnp.stack([ext[:, CHUNK - 1 - iq:CHUNK - 1 - iq + BAND] for iq in range(CHUNK)],
                     axis=1)
    nq = sub // CHUNK
    w = sub + BAND_PAST
    rows = []
    for m in range(nq):
        left = jnp.full((N_HEADS, CHUNK, m * CHUNK), NEG_INF, F32)
        right = jnp.full((N_HEADS, CHUNK, w - BAND - m * CHUNK), NEG_INF, F32)
        rows.append(jnp.concatenate([left, bias, right], axis=2))
    return jnp.concatenate(rows, axis=1)


def _band_attention(q_arr, q_col, kprev, kp_col, kcur, kc_col, vprev, vp_col, vcur, vc_col,
                    rel_bias, *, qb, sub, pos0, seq_len, prev_is_shifted):
    b, sq = q_arr.shape[0], q_arr.shape[1]
    bias = _band_bias(rel_bias, sub)
    w = sub + BAND_PAST
    if prev_is_shifted:
        assert qb == BAND_PAST
        prev_map = lambda col: (lambda bi, i: (bi, jnp.maximum(i - 1, 0), col))
    else:
        prev_map = lambda col: (lambda bi, i: (bi, 0, col))
    cur_map = lambda col: (lambda bi, i: (bi, i, col))
    kern = functools.partial(_attn_kernel, qb=qb, sub=sub, pos0=pos0, seq_len=seq_len)
    return pl.pallas_call(
        kern,
        grid=(b, sq // qb),
        in_specs=[pl.BlockSpec((1, qb, D_MODEL), cur_map(q_col)),
                  pl.BlockSpec((1, BAND_PAST, D_MODEL), prev_map(kp_col)),
                  pl.BlockSpec((1, qb, D_MODEL), cur_map(kc_col)),
                  pl.BlockSpec((1, BAND_PAST, D_MODEL), prev_map(vp_col)),
                  pl.BlockSpec((1, qb, D_MODEL), cur_map(vc_col)),
                  pl.BlockSpec((N_HEADS, sub, w), lambda bi, i: (0, 0, 0))],
        out_specs=pl.BlockSpec((1, qb, D_MODEL), lambda bi, i: (bi, i, 0)),
        out_shape=jax.ShapeDtypeStruct((b, sq, D_MODEL), BF16),
        scratch_shapes=[pltpu.VMEM((BAND_PAST + qb, D_MODEL), BF16),
                        pltpu.VMEM((BAND_PAST + qb, D_MODEL), BF16)],
        compiler_params=_cparams(("arbitrary", "arbitrary")),
        name="band_attn",
    )(q_arr, kprev, kcur, vprev, vcur, bias)


def _proj_ln_kernel(o_ref, w_ref, x_ref, g_ref, b_ref, y_ref, yb_ref):
    y = jnp.dot(o_ref[...], w_ref[...], preferred_element_type=F32)
    z = _layer_norm(ALPHA * x_ref[...] + y, g_ref[...], b_ref[...])
    y_ref[...] = z
    yb_ref[...] = z.astype(BF16)


def _proj_ln(o, w, x, g, b, tm):
    t = x.shape[0]
    tm = min(tm, t)
    row = lambda i: (i, 0)
    fixed = lambda i: (0, 0)
    return pl.pallas_call(
        _proj_ln_kernel,
        grid=(t // tm,),
        in_specs=[pl.BlockSpec((tm, D_MODEL), row),
                  pl.BlockSpec((D_MODEL, D_MODEL), fixed),
                  pl.BlockSpec((tm, D_MODEL), row),
                  pl.BlockSpec((1, D_MODEL), fixed),
                  pl.BlockSpec((1, D_MODEL), fixed)],
        out_specs=[pl.BlockSpec((tm, D_MODEL), row), pl.BlockSpec((tm, D_MODEL), row)],
        out_shape=[jax.ShapeDtypeStruct((t, D_MODEL), F32),
                   jax.ShapeDtypeStruct((t, D_MODEL), BF16)],
        compiler_params=_cparams(("arbitrary",)),
        name="proj_ln",
    )(o, w, x, g, b)


def _pool_kernel(x_ref, hist_ref, win_ref, wg_ref, sc_ref, g_ref, b_ref,
                 y_ref, yb_ref, st_ref, ext, *, tm, pos0):
    i = pl.program_id(1)

    @pl.when(i == 0)
    def _():
        ext[0:POOL_CARRY, :] = hist_ref[0]

    x = x_ref[0]
    u = jnp.dot(x.astype(BF16), win_ref[...], preferred_element_type=F32)
    ext[POOL_CARRY:POOL_CARRY + tm, :] = u
    row = lax.broadcasted_iota(jnp.int32, (tm, POOL_GROUP_DIM), 0)
    pos = (pos0 + i * tm + row + 1).astype(F32)
    zs = []
    for gi, wlen in enumerate(POOL_WINDOWS):
        c0 = gi * POOL_GROUP_DIM
        acc = u[:, c0:c0 + POOL_GROUP_DIM]
        for sft in range(1, wlen):
            acc = acc + ext[POOL_CARRY - sft:POOL_CARRY - sft + tm, c0:c0 + POOL_GROUP_DIM]
        mean = acc / jnp.minimum(float(wlen), pos)
        yg = (mean - u[:, c0:c0 + POOL_GROUP_DIM]).astype(BF16)
        zs.append(jnp.dot(yg, wg_ref[gi], preferred_element_type=F32))
    z = jnp.concatenate(zs, axis=1) * sc_ref[...]
    out = _layer_norm(ALPHA * x + z, g_ref[...], b_ref[...])
    y_ref[0] = out
    yb_ref[0] = out.astype(BF16)
    tail = ext[tm:tm + POOL_CARRY, :]
    st_ref[0] = tail
    ext[0:POOL_CARRY, :] = tail


def _pool_mixer_ln(x, hist, w_in, w_grp, scale, g, b, *, tm, pos0):
    bsz, s, _ = x.shape
    tm = min(tm, s)
    kern = functools.partial(_pool_kernel, tm=tm, pos0=pos0)
    blk = lambda bi, i: (bi, i, 0)
    per_b = lambda bi, i: (bi, 0, 0)
    fixed2 = lambda bi, i: (0, 0)
    return pl.pallas_call(
        kern,
        grid=(bsz, s // tm),
        in_specs=[pl.BlockSpec((1, tm, D_MODEL), blk),
                  pl.BlockSpec((1, POOL_CARRY, D_MODEL), per_b),
                  pl.BlockSpec((D_MODEL, D_MODEL), fixed2),
                  pl.BlockSpec((N_POOL_GROUPS, POOL_GROUP_DIM, POOL_GROUP_DIM),
                               lambda bi, i: (0, 0, 0)),
                  pl.BlockSpec((1, D_MODEL), fixed2),
                  pl.BlockSpec((1, D_MODEL), fixed2),
                  pl.BlockSpec((1, D_MODEL), fixed2)],
        out_specs=[pl.BlockSpec((1, tm, D_MODEL), blk),
                   pl.BlockSpec((1, tm, D_MODEL), blk),
                   pl.BlockSpec((1, POOL_CARRY, D_MODEL), per_b)],
        out_shape=[jax.ShapeDtypeStruct((bsz, s, D_MODEL), F32),
                   jax.ShapeDtypeStruct((bsz, s, D_MODEL), BF16),
                   jax.ShapeDtypeStruct((bsz, POOL_CARRY, D_MODEL), F32)],
        scratch_shapes=[pltpu.VMEM((POOL_CARRY + tm, D_MODEL), F32)],
        compiler_params=_cparams(("arbitrary", "arbitrary")),
        name="pool_mixer",
    )(x, hist, w_in, w_grp, scale, g, b)


def _extract_top(vals, n, dst_ref):
    for it in range(n):
        m = jnp.max(vals, axis=0, keepdims=True)
        dst_ref[it:it + 1, :] = m
        if it + 1 < n:
            vals = jnp.where(vals == m, -jnp.inf, vals)


def _route_kernel(x_ref, wq_ref, sk_ref, a_ref, b_ref, kap_ref, qt, r1, r2, cl, *, tb):
    qt[...] = lax.dot_general(wq_ref[...], x_ref[...], (((1,), (1,)), ((), ())),
                              preferred_element_type=F32)

    def head_body(h, carry):
        q1 = qt[pl.ds(pl.multiple_of(h * PEER_KEY_DIM, PEER_KEY_DIM), PEER_HALF), :]
        q2 = qt[pl.ds(pl.multiple_of(h * PEER_KEY_DIM + PEER_HALF, PEER_HALF), PEER_HALF), :]
        s1 = jnp.dot(sk_ref[h, 0], q1.astype(BF16), preferred_element_type=F32)
        s2 = jnp.dot(sk_ref[h, 1], q2.astype(BF16), preferred_element_type=F32)
        _extract_top(s1, PEER_TOPK, r1)
        _extract_top(s2, PEER_TOPK, r2)
        lo2 = r2[0:SUBLANES, :]
        tiles = [r1[a:a + 1, :] + lo2 for a in range(SUBLANES)]
        tiles.append(r1[0:1, :] + r2[SUBLANES:PEER_TOPK, :])
        tiles.append(r1[SUBLANES:PEER_TOPK, :] + r2[0:1, :])
        cand = jnp.concatenate(tiles, axis=0)
        _extract_top(cand, PEER_TOPK + 1, cl)
        c_top = cl[0:1, :]
        tau = 0.5 * (cl[PEER_TOPK - 1:PEER_TOPK, :] + cl[PEER_TOPK:PEER_TOPK + 1, :])
        zsum = jnp.sum(jnp.where(cand > tau, jnp.exp(cand - c_top), 0.0), axis=0, keepdims=True)
        inv_z = 1.0 / zsum
        a = jnp.where(s1 >= r1[PEER_TOPK - 1:PEER_TOPK, :], jnp.exp(s1 - r1[0:1, :]), 0.0)
        b = jnp.where(s2 >= r2[PEER_TOPK - 1:PEER_TOPK, :], jnp.exp(s2 - r2[0:1, :]), 0.0)
        a_ref[h] = a
        b_ref[h] = b * (0.5 * inv_z)
        kap_ref[h] = jnp.exp(tau - c_top) * (0.5 * inv_z)
        return carry

    lax.fori_loop(0, PEER_HEADS, head_body, 0, unroll=ROUTE_HEAD_UNROLL)


def _peer_route(xb, wq_t, sk, tb):
    t = xb.shape[0]
    tb = min(tb, t)
    kern = functools.partial(_route_kernel, tb=tb)
    return pl.pallas_call(
        kern,
        grid=(t // tb,),
        in_specs=[pl.BlockSpec((tb, D_MODEL), lambda i: (i, 0)),
                  pl.BlockSpec((PEER_HEADS * PEER_KEY_DIM, D_MODEL), lambda i: (0, 0)),
                  pl.BlockSpec((PEER_HEADS, 2, N_KEYS, PEER_HALF), lambda i: (0, 0, 0, 0))],
        out_specs=[pl.BlockSpec((PEER_HEADS, N_KEYS, tb), lambda i: (0, 0, i)),
                   pl.BlockSpec((PEER_HEADS, N_KEYS, tb), lambda i: (0, 0, i)),
                   pl.BlockSpec((PEER_HEADS, 1, tb), lambda i: (0, 0, i))],
        out_shape=[jax.ShapeDtypeStruct((PEER_HEADS, N_KEYS, t), F32),
                   jax.ShapeDtypeStruct((PEER_HEADS, N_KEYS, t), F32),
                   jax.ShapeDtypeStruct((PEER_HEADS, 1, t), F32)],
        scratch_shapes=[pltpu.VMEM((PEER_HEADS * PEER_KEY_DIM, tb), F32),
                        pltpu.VMEM((PEER_TOPK, tb), F32),
                        pltpu.VMEM((PEER_TOPK, tb), F32),
                        pltpu.VMEM((3 * SUBLANES, tb), F32)],
        compiler_params=_cparams(("arbitrary",)),
        name="peer_route",
    )(xb, wq_t, sk)


def _peer_kernel(xb_ref, x_ref, u_ref, vt_ref, a_ref, b_ref, kap_ref,
                 g_ref, bt_ref, y_ref, acc, ht, gt, *, tb, eb):
    e = pl.program_id(1)
    ni = eb // N_KEYS
    nlt = tb // LANES
    inv_sqrt2 = 2.0 ** -0.5

    @pl.when(e == 0)
    def _():
        acc[...] = jnp.zeros_like(acc)

    cgw = min(MXU_COLS, tb)
    ncg = tb // cgw
    lpc = cgw // LANES

    def stage_a(half, slot, cg):
        c0 = cg * cgw
        ht[slot, :, c0:c0 + cgw] = lax.dot_general(
            u_ref[half * eb:(half + 1) * eb, :], xb_ref[c0:c0 + cgw, :],
            (((1,), (1,)), ((), ())), preferred_element_type=F32)

    def stage_b(slot, row0, cg):
        for ii in range(ni):
            for lt in range(cg * lpc, (cg + 1) * lpc):
                l0 = lt * LANES
                wsum = jnp.zeros((N_KEYS, LANES), F32)
                for h in range(PEER_HEADS):
                    p = a_ref[h, row0 + ii:row0 + ii + 1, l0:l0 + LANES] * b_ref[h, :, l0:l0 + LANES]
                    wsum = wsum + jnp.where(p > kap_ref[h, :, l0:l0 + LANES], p, 0.0)
                hh = ht[slot, ii * N_KEYS:(ii + 1) * N_KEYS, l0:l0 + LANES]
                gval = wsum * (hh * (1.0 + lax.erf(hh * inv_sqrt2)))
                gt[slot, ii * N_KEYS:(ii + 1) * N_KEYS, l0:l0 + LANES] = gval.astype(BF16)

    def stage_c(slot, half, cg):
        c0 = cg * cgw
        for mc in range(D_MODEL // PEER_ROW_CHUNK):
            r0 = mc * PEER_ROW_CHUNK
            acc[r0:r0 + PEER_ROW_CHUNK, c0:c0 + cgw] += jnp.dot(
                vt_ref[r0:r0 + PEER_ROW_CHUNK, half * eb:(half + 1) * eb],
                gt[slot, :, c0:c0 + cgw], preferred_element_type=F32)

    for half in range(2):
        for cg in range(ncg):
            stage_a(half, half, cg)
    for half in range(2):
        for cg in range(ncg):
            stage_b(half, half * ni, cg)
            stage_c(half, half, cg)

    @pl.when(e == pl.num_programs(1) - 1)
    def _():
        for lt in range(nlt):
            l0 = lt * LANES
            z = ALPHA * x_ref[l0:l0 + LANES, :] + acc[:, l0:l0 + LANES].T
            y_ref[l0:l0 + LANES, :] = _layer_norm(z, g_ref[...], bt_ref[...])


def _peer_dense_ln(xb, x, u, vt, a, b, kap, g, bt, *, tb, eb):
    t = x.shape[0]
    tb = min(tb, t)
    assert 2 * eb == SUBLANES * N_KEYS
    n_e = N_EXPERTS // (2 * eb)
    kern = functools.partial(_peer_kernel, tb=tb, eb=eb)
    row = lambda i, e: (i, 0)
    fixed = lambda i, e: (0, 0)
    per_tok = lambda i, e: (0, 0, i)
    return pl.pallas_call(
        kern,
        grid=(t // tb, n_e),
        in_specs=[pl.BlockSpec((tb, D_MODEL), row),
                  pl.BlockSpec((tb, D_MODEL), row),
                  pl.BlockSpec((2 * eb, D_MODEL), lambda i, e: (e, 0)),
                  pl.BlockSpec((D_MODEL, 2 * eb), lambda i, e: (0, e)),
                  pl.BlockSpec((PEER_HEADS, SUBLANES, tb), lambda i, e: (0, e, i)),
                  pl.BlockSpec((PEER_HEADS, N_KEYS, tb), per_tok),
                  pl.BlockSpec((PEER_HEADS, 1, tb), per_tok),
                  pl.BlockSpec((1, D_MODEL), fixed),
                  pl.BlockSpec((1, D_MODEL), fixed)],
        out_specs=pl.BlockSpec((tb, D_MODEL), row),
        out_shape=jax.ShapeDtypeStruct((t, D_MODEL), F32),
        scratch_shapes=[pltpu.VMEM((D_MODEL, tb), F32),
                        pltpu.VMEM((2, eb, tb), F32),
                        pltpu.VMEM((2, eb, tb), BF16)],
        compiler_params=_cparams(("arbitrary", "arbitrary")),
        name="peer_dense",
    )(xb, x, u, vt, a, b, kap, g, bt)


def _ple_kernel(x_ref, p_ref, wg_ref, wp_ref, y_ref):
    x = x_ref[...]
    gate = jax.nn.sigmoid(jnp.dot(x.astype(BF16), wg_ref[...], preferred_element_type=F32))
    emb = jnp.dot(p_ref[...].astype(BF16), wp_ref[...], preferred_element_type=F32)
    y_ref[...] = x + gate * emb


def _ple(x, p, wg, wp, tm):
    t = x.shape[0]
    tm = min(tm, t)
    row = lambda i: (i, 0)
    fixed = lambda i: (0, 0)
    return pl.pallas_call(
        _ple_kernel,
        grid=(t // tm,),
        in_specs=[pl.BlockSpec((tm, D_MODEL), row),
                  pl.BlockSpec((tm, PLE_DIM), row),
                  pl.BlockSpec((D_MODEL, D_MODEL), fixed),
                  pl.BlockSpec((PLE_DIM, D_MODEL), fixed)],
        out_specs=pl.BlockSpec((tm, D_MODEL), row),
        out_shape=jax.ShapeDtypeStruct((t, D_MODEL), F32),
        compiler_params=_cparams(("arbitrary",)),
        name="ple",
    )(x, p, wg, wp)


def _channel_mixer(x1, x1b, p, wts, *, tm, route_tb, peer_tb, peer_eb):
    a, b, kap = _peer_route(x1b, wts["wq_t"], wts["sk"], route_tb)
    x2 = _peer_dense_ln(x1b, x1, wts["u"], wts["vt"], a, b, kap, wts["g1"], wts["b1"],
                        tb=peer_tb, eb=peer_eb)
    return _ple(x2, p, wts["wg"], wts["wp"], tm)


def _attention_mixer(x, hist_k, hist_v, pos0, wts, *, tm, qb, sub):
    bsz, s, _ = x.shape
    xf = x.reshape(bsz * s, D_MODEL)
    qkv = _matmul(xf, wts["wqkv"], BF16, tm).reshape(bsz, s, 3 * D_MODEL)
    if hist_k is None:
        o = _band_attention(qkv, 0, qkv, 1, qkv, 1, qkv, 2, qkv, 2, wts["rel_bias"],
                            qb=qb, sub=sub, pos0=pos0, seq_len=s, prev_is_shifted=True)
    else:
        sp = -(-s // CHUNK) * CHUNK
        qkv_p = jnp.pad(qkv, ((0, 0), (0, sp - s), (0, 0)))
        o = _band_attention(qkv_p, 0, hist_k, 0, qkv_p, 1, hist_v, 0, qkv_p, 2, wts["rel_bias"],
                            qb=sp, sub=CHUNK, pos0=pos0, seq_len=s, prev_is_shifted=False)
        o = o[:, :s]
    x1, x1b = _proj_ln(o.reshape(bsz * s, D_MODEL), wts["wo"], xf, wts["g0"], wts["b0"], tm)
    return x1, x1b


def _new_kv(x_rows, wts, tm):
    return _matmul(x_rows, wts["wkv"], F32, tm)


def kernel(x_prompt, x_sample, cache_att_k, cache_att_v, state_pool, p_prompt, p_sample,
           att_w_qkv, att_w_o, att_rel_bias, pool_w_in, pool_w_grp, pool_scale,
           peer_w_query, peer_sub_keys, peer_u, peer_v, ln_g, ln_b, ple_w_proj, ple_w_gate):
    bp, sp_len, _ = x_prompt.shape
    bs, ss_len, _ = x_sample.shape
    tm = 512
    cfg_p = dict(tm=tm, route_tb=256, peer_tb=512, peer_eb=512)
    cfg_s = dict(tm=tm, route_tb=bs * ss_len, peer_tb=bs * ss_len, peer_eb=512)
    xp, xs = x_prompt, x_sample
    kp_out, vp_out, ks_out, vs_out, pp_out, ps_out = [], [], [], [], [], []
    for i in range(DEPTH):
        j = i // 2
        wts = {
            "wq_t": peer_w_query[i].T.astype(BF16),
            "sk": peer_sub_keys[i].astype(BF16),
            "u": peer_u[i].astype(BF16),
            "vt": peer_v[i].astype(BF16).T,
            "g0": ln_g[i, 0].reshape(1, D_MODEL), "b0": ln_b[i, 0].reshape(1, D_MODEL),
            "g1": ln_g[i, 1].reshape(1, D_MODEL), "b1": ln_b[i, 1].reshape(1, D_MODEL),
            "wg": ple_w_gate[i].astype(BF16),
            "wp": ple_w_proj[i].astype(BF16),
        }
        if i % 2 == 0:
            wts["wqkv"] = att_w_qkv[j].astype(BF16)
            wts["wkv"] = wts["wqkv"][:, D_MODEL:]
            wts["wo"] = att_w_o[j].astype(BF16)
            wts["rel_bias"] = att_rel_bias[j]
            keep = min(BAND_PAST, sp_len)
            x1p, x1pb = _attention_mixer(xp, None, None, 0, wts, tm=tm, qb=BAND_PAST, sub=2 * CHUNK)
            kv_p = _new_kv(xp[:, sp_len - keep:].reshape(bp * keep, D_MODEL), wts, tm)
            kv_p = kv_p.reshape(bp, keep, 2, N_HEADS, HEAD_DIM)
            kp_out.append(kv_p[:, :, 0])
            vp_out.append(kv_p[:, :, 1])
            hk = cache_att_k[j].astype(BF16).reshape(bs, -1, D_MODEL)
            hv = cache_att_v[j].astype(BF16).reshape(bs, -1, D_MODEL)
            pad = BAND_PAST - hk.shape[1]
            hk = jnp.pad(hk, ((0, 0), (pad, 0), (0, 0)))
            hv = jnp.pad(hv, ((0, 0), (pad, 0), (0, 0)))
            x1s, x1sb = _attention_mixer(xs, hk, hv, PAST_LEN, wts, tm=tm, qb=CHUNK, sub=CHUNK)
            kv_s = _new_kv(xs.reshape(bs * ss_len, D_MODEL), wts, tm)
            kv_s = kv_s.reshape(bs, ss_len, 2, N_HEADS, HEAD_DIM)
            ks_out.append(kv_s[:, :, 0])
            vs_out.append(kv_s[:, :, 1])
        else:
            w_in = pool_w_in[j].astype(BF16)
            w_grp = pool_w_grp[j].astype(BF16)
            scale = pool_scale[j].reshape(1, D_MODEL)
            zero_hist = jnp.zeros((bp, POOL_CARRY, D_MODEL), F32)
            x1p, x1pb, st_p = _pool_mixer_ln(xp, zero_hist, w_in, w_grp, scale, wts["g0"], wts["b0"],
                                            tm=tm, pos0=0)
            hist_s = jnp.pad(state_pool[j], ((0, 0), (POOL_CARRY - POOL_HIST, 0), (0, 0)))
            x1s, x1sb, st_s = _pool_mixer_ln(xs, hist_s, w_in, w_grp, scale, wts["g0"], wts["b0"],
                                            tm=tm, pos0=PAST_LEN)
            x1p = x1p.reshape(bp * sp_len, D_MODEL)
            x1pb = x1pb.reshape(bp * sp_len, D_MODEL)
            x1s = x1s.reshape(bs * ss_len, D_MODEL)
            x1sb = x1sb.reshape(bs * ss_len, D_MODEL)
            pp_out.append(st_p[:, POOL_CARRY - POOL_HIST:])
            ps_out.append(st_s[:, POOL_CARRY - POOL_HIST:])
        xp = _channel_mixer(x1p, x1pb, p_prompt[i].reshape(bp * sp_len, PLE_DIM), wts,
                            **cfg_p).reshape(bp, sp_len, D_MODEL)
        xs = _channel_mixer(x1s, x1sb, p_sample[i].reshape(bs * ss_len, PLE_DIM), wts,
                            **cfg_s).reshape(bs, ss_len, D_MODEL)
    return (xp, xs, jnp.stack(kp_out), jnp.stack(vp_out), jnp.stack(ks_out), jnp.stack(vs_out),
            jnp.stack(pp_out), jnp.stack(ps_out))
```

```python
import functools

import jax
import jax.numpy as jnp
import numpy as np
from jax import lax
from jax.experimental import pallas as pl
from jax.experimental.pallas import tpu as pltpu

F32 = jnp.float32
BF16 = jnp.bfloat16

D_MODEL = 2048
DEPTH = 2
PAST_LEN = 4096
CHUNK = 64
N_HEADS = 16
HEAD_DIM = D_MODEL // N_HEADS
LEFT_CHUNKS = 8
BAND_PAST = LEFT_CHUNKS * CHUNK
BAND = BAND_PAST + CHUNK
REL_MAX = 256
REL_MIN = -(CHUNK - 1)
POOL_WINDOWS = (2, 4, 8, 16)
N_POOL_GROUPS = len(POOL_WINDOWS)
POOL_GROUP_DIM = D_MODEL // N_POOL_GROUPS
POOL_HIST = max(POOL_WINDOWS) - 1
POOL_CARRY = 16
PEER_HEADS = 8
PEER_KEY_DIM = 256
PEER_HALF = PEER_KEY_DIM // 2
N_KEYS = 128
N_EXPERTS = N_KEYS * N_KEYS
PEER_TOPK = 16
PLE_DIM = 256
PEER_ROW_CHUNK = 512
ROUTE_HEAD_UNROLL = 4
ALPHA = (2 * DEPTH) ** 0.25
LN_EPS = 1e-5
NEG_INF = -1e30

LANES = 128
SUBLANES = 8
MXU_COLS = 256
PEER_STEP_EXPERTS = SUBLANES * N_KEYS
VMEM_LIMIT_BYTES = 56 * 1024 * 1024


def _cparams(sem):
    return pltpu.CompilerParams(dimension_semantics=sem, vmem_limit_bytes=VMEM_LIMIT_BYTES)


def _layer_norm(z, g, b):
    mu = jnp.mean(z, axis=-1, keepdims=True)
    zc = z - mu
    var = jnp.mean(zc * zc, axis=-1, keepdims=True)
    return zc * lax.rsqrt(var + LN_EPS) * g + b


def _mm_kernel(x_ref, w_ref, o_ref):
    x = x_ref[...].astype(BF16)
    o_ref[...] = jnp.dot(x, w_ref[...], preferred_element_type=F32).astype(o_ref.dtype)


def _matmul(x, w, out_dtype, tm):
    t, k = x.shape
    n = w.shape[1]
    tn = min(n, D_MODEL)
    tm = min(tm, t)
    return pl.pallas_call(
        _mm_kernel,
        grid=(n // tn, t // tm),
        in_specs=[pl.BlockSpec((tm, k), lambda j, i: (i, 0)),
                  pl.BlockSpec((k, tn), lambda j, i: (0, j))],
        out_specs=pl.BlockSpec((tm, tn), lambda j, i: (i, j)),
        out_shape=jax.ShapeDtypeStruct((t, n), out_dtype),
        compiler_params=_cparams(("arbitrary", "arbitrary")),
        name="proj",
    )(x, w)


def _attn_kernel(q_ref, kp_ref, kc_ref, vp_ref, vc_ref, bias_ref, o_ref, kwin, vwin,
                 *, qb, sub, pos0, seq_len):
    i = pl.program_id(1)
    w = sub + BAND_PAST
    kwin[0:BAND_PAST, :] = kp_ref[0]
    kwin[BAND_PAST:BAND_PAST + qb, :] = kc_ref[0]
    vwin[0:BAND_PAST, :] = vp_ref[0]
    vwin[BAND_PAST:BAND_PAST + qb, :] = vc_ref[0]
    scale = HEAD_DIM ** -0.5
    col = lax.broadcasted_iota(jnp.int32, (sub, w), 1)

    def sub_body(s, carry):
        r0 = pl.multiple_of(s * sub, sub)
        kpos = (pos0 - BAND_PAST) + i * qb + r0 + col
        valid = (kpos >= 0) & (kpos < pos0 + seq_len)
        for h in range(N_HEADS):
            c0 = h * HEAD_DIM
            q = q_ref[0, pl.ds(r0, sub), c0:c0 + HEAD_DIM]
            k = kwin[pl.ds(r0, w), c0:c0 + HEAD_DIM]
            v = vwin[pl.ds(r0, w), c0:c0 + HEAD_DIM]
            sc = lax.dot_general(q, k, (((1,), (1,)), ((), ())), preferred_element_type=F32)
            sc = sc * scale + bias_ref[h]
            sc = jnp.where(valid, sc, NEG_INF)
            m = jnp.max(sc, axis=-1, keepdims=True)
            p = jnp.exp(sc - m)
            l = jnp.sum(p, axis=-1, keepdims=True)
            o = jnp.dot(p.astype(BF16), v, preferred_element_type=F32) / l
            o_ref[0, pl.ds(r0, sub), c0:c0 + HEAD_DIM] = o.astype(o_ref.dtype)
        return carry

    lax.fori_loop(0, qb // sub, sub_body, 0)


def _band_bias(rel_bias, sub):
    n_rel = REL_MAX - REL_MIN + 1
    rev = rel_bias.astype(F32)[:, ::-1]
    lead = BAND_PAST - REL_MAX - REL_MIN
    ext = jnp.concatenate([jnp.broadcast_to(rev[:, :1], (N_HEADS, lead)), rev], axis=1)
    assert ext.shape[1] == CHUNK - 1 + BAND and lead + n_rel == ext.shape[1]
    bias = jnp.stack([ext[:, CHUNK - 1 - iq:CHUNK - 1 - iq + BAND] for iq in range(CHUNK)],
                     axis=1)
    nq = sub // CHUNK
    w = sub + BAND_PAST
    rows = []
    for m in range(nq):
        left = jnp.full((N_HEADS, CHUNK, m * CHUNK), NEG_INF, F32)
        right = jnp.full((N_HEADS, CHUNK, w - BAND - m * CHUNK), NEG_INF, F32)
        rows.append(jnp.concatenate([left, bias, right], axis=2))
    return jnp.concatenate(rows, axis=1)


def _band_attention(q_arr, q_col, kprev, kp_col, kcur, kc_col, vprev, vp_col, vcur, vc_col,
                    rel_bias, *, qb, sub, pos0, seq_len, prev_is_shifted):
    b, sq = q_arr.shape[0], q_arr.shape[1]
    bias = _band_bias(rel_bias, sub)
    w = sub + BAND_PAST
    if prev_is_shifted:
        assert qb == BAND_PAST
        prev_map = lambda col: (lambda bi, i: (bi, jnp.maximum(i - 1, 0), col))
    else:
        prev_map = lambda col: (lambda bi, i: (bi, 0, col))
    cur_map = lambda col: (lambda bi, i: (bi, i, col))
    kern = functools.partial(_attn_kernel, qb=qb, sub=sub, pos0=pos0, seq_len=seq_len)
    return pl.pallas_call(
        kern,
        grid=(b, sq // qb),
        in_specs=[pl.BlockSpec((1, qb, D_MODEL), cur_map(q_col)),
                  pl.BlockSpec((1, BAND_PAST, D_MODEL), prev_map(kp_col)),
                  pl.BlockSpec((1, qb, D_MODEL), cur_map(kc_col)),
                  pl.BlockSpec((1, BAND_PAST, D_MODEL), prev_map(vp_col)),
                  pl.BlockSpec((1, qb, D_MODEL), cur_map(vc_col)),
                  pl.BlockSpec((N_HEADS, sub, w), lambda bi, i: (0, 0, 0))],
        out_specs=pl.BlockSpec((1, qb, D_MODEL), lambda bi, i: (bi, i, 0)),
        out_shape=jax.ShapeDtypeStruct((b, sq, D_MODEL), BF16),
        scratch_shapes=[pltpu.VMEM((BAND_PAST + qb, D_MODEL), BF16),
                        pltpu.VMEM((BAND_PAST + qb, D_MODEL), BF16)],
        compiler_params=_cparams(("arbitrary", "arbitrary")),
        name="band_attn",
    )(q_arr, kprev, kcur, vprev, vcur, bias)


def _proj_ln_kernel(o_ref, w_ref, x_ref, g_ref, b_ref, y_ref, yb_ref):
    y = jnp.dot(o_ref[...], w_ref[...], preferred_element_type=F32)
    z = _layer_norm(ALPHA * x_ref[...] + y, g_ref[...], b_ref[...])
    y_ref[...] = z
    yb_ref[...] = z.astype(BF16)


def _proj_ln(o, w, x, g, b, tm):
    t = x.shape[0]
    tm = min(tm, t)
    row = lambda i: (i, 0)
    fixed = lambda i: (0, 0)
    return pl.pallas_call(
        _proj_ln_kernel,
        grid=(t // tm,),
        in_specs=[pl.BlockSpec((tm, D_MODEL), row),
                  pl.BlockSpec((D_MODEL, D_MODEL), fixed),
                  pl.BlockSpec((tm, D_MODEL), row),
                  pl.BlockSpec((1, D_MODEL), fixed),
                  pl.BlockSpec((1, D_MODEL), fixed)],
        out_specs=[pl.BlockSpec((tm, D_MODEL), row), pl.BlockSpec((tm, D_MODEL), row)],
        out_shape=[jax.ShapeDtypeStruct((t, D_MODEL), F32),
                   jax.ShapeDtypeStruct((t, D_MODEL), BF16)],
        compiler_params=_cparams(("arbitrary",)),
        name="proj_ln",
    )(o, w, x, g, b)


def _pool_kernel(x_ref, hist_ref, win_ref, wg_ref, sc_ref, g_ref, b_ref,
                 y_ref, yb_ref, st_ref, ext, *, tm, pos0):
    i = pl.program_id(1)

    @pl.when(i == 0)
    def _():
        ext[0:POOL_CARRY, :] = hist_ref[0]

    x = x_ref[0]
    u = jnp.dot(x.astype(BF16), win_ref[...], preferred_element_type=F32)
    ext[POOL_CARRY:POOL_CARRY + tm, :] = u
    row = lax.broadcasted_iota(jnp.int32, (tm, POOL_GROUP_DIM), 0)
    pos = (pos0 + i * tm + row + 1).astype(F32)
    zs = []
    for gi, wlen in enumerate(POOL_WINDOWS):
        c0 = gi * POOL_GROUP_DIM
        acc = u[:, c0:c0 + POOL_GROUP_DIM]
        for sft in range(1, wlen):
            acc = acc + ext[POOL_CARRY - sft:POOL_CARRY - sft + tm, c0:c0 + POOL_GROUP_DIM]
        mean = acc / jnp.minimum(float(wlen), pos)
        yg = (mean - u[:, c0:c0 + POOL_GROUP_DIM]).astype(BF16)
        zs.append(jnp.dot(yg, wg_ref[gi], preferred_element_type=F32))
    z = jnp.concatenate(zs, axis=1) * sc_ref[...]
    out = _layer_norm(ALPHA * x + z, g_ref[...], b_ref[...])
    y_ref[0] = out
    yb_ref[0] = out.astype(BF16)
    tail = ext[tm:tm + POOL_CARRY, :]
    st_ref[0] = tail
    ext[0:POOL_CARRY, :] = tail


def _pool_mixer_ln(x, hist, w_in, w_grp, scale, g, b, *, tm, pos0):
    bsz, s, _ = x.shape
    tm = min(tm, s)
    kern = functools.partial(_pool_kernel, tm=tm, pos0=pos0)
    blk = lambda bi, i: (bi, i, 0)
    per_b = lambda bi, i: (bi, 0, 0)
    fixed2 = lambda bi, i: (0, 0)
    return pl.pallas_call(
        kern,
        grid=(bsz, s // tm),
        in_specs=[pl.BlockSpec((1, tm, D_MODEL), blk),
                  pl.BlockSpec((1, POOL_CARRY, D_MODEL), per_b),
                  pl.BlockSpec((D_MODEL, D_MODEL), fixed2),
                  pl.BlockSpec((N_POOL_GROUPS, POOL_GROUP_DIM, POOL_GROUP_DIM),
                               lambda bi, i: (0, 0, 0)),
                  pl.BlockSpec((1, D_MODEL), fixed2),
                  pl.BlockSpec((1, D_MODEL), fixed2),
                  pl.BlockSpec((1, D_MODEL), fixed2)],
        out_specs=[pl.BlockSpec((1, tm, D_MODEL), blk),
                   pl.BlockSpec((1, tm, D_MODEL), blk),
                   pl.BlockSpec((1, POOL_CARRY, D_MODEL), per_b)],
        out_shape=[jax.ShapeDtypeStruct((bsz, s, D_MODEL), F32),
                   jax.ShapeDtypeStruct((bsz, s, D_MODEL), BF16),
                   jax.ShapeDtypeStruct((bsz, POOL_CARRY, D_MODEL), F32)],
        scratch_shapes=[pltpu.VMEM((POOL_CARRY + tm, D_MODEL), F32)],
        compiler_params=_cparams(("arbitrary", "arbitrary")),
        name="pool_mixer",
    )(x, hist, w_in, w_grp, scale, g, b)


def _extract_top(vals, n, dst_ref):
    for it in range(n):
        m = jnp.max(vals, axis=0, keepdims=True)
        dst_ref[it:it + 1, :] = m
        if it + 1 < n:
            vals = jnp.where(vals == m, -jnp.inf, vals)


def _route_kernel(x_ref, wq_ref, sk_ref, a_ref, b_ref, kap_ref, qt, r1, r2, cl, *, tb):
    qt[...] = lax.dot_general(wq_ref[...], x_ref[...], (((1,), (1,)), ((), ())),
                              preferred_element_type=F32)

    def head_body(h, carry):
        q1 = qt[pl.ds(pl.multiple_of(h * PEER_KEY_DIM, PEER_KEY_DIM), PEER_HALF), :]
        q2 = qt[pl.ds(pl.multiple_of(h * PEER_KEY_DIM + PEER_HALF, PEER_HALF), PEER_HALF), :]
        s1 = jnp.dot(sk_ref[h, 0], q1.astype(BF16), preferred_element_type=F32)
        s2 = jnp.dot(sk_ref[h, 1], q2.astype(BF16), preferred_element_type=F32)
        _extract_top(s1, PEER_TOPK, r1)
        _extract_top(s2, PEER_TOPK, r2)
        lo2 = r2[0:SUBLANES, :]
        tiles = [r1[a:a + 1, :] + lo2 for a in range(SUBLANES)]
        tiles.append(r1[0:1, :] + r2[SUBLANES:PEER_TOPK, :])
        tiles.append(r1[SUBLANES:PEER_TOPK, :] + r2[0:1, :])
        cand = jnp.concatenate(tiles, axis=0)
        _extract_top(cand, PEER_TOPK + 1, cl)
        c_top = cl[0:1, :]
        tau = 0.5 * (cl[PEER_TOPK - 1:PEER_TOPK, :] + cl[PEER_TOPK:PEER_TOPK + 1, :])
        zsum = jnp.sum(jnp.where(cand > tau, jnp.exp(cand - c_top), 0.0), axis=0, keepdims=True)
        inv_z = 1.0 / zsum
        a = jnp.where(s1 >= r1[PEER_TOPK - 1:PEER_TOPK, :], jnp.exp(s1 - r1[0:1, :]), 0.0)
        b = jnp.where(s2 >= r2[PEER_TOPK - 1:PEER_TOPK, :], jnp.exp(s2 - r2[0:1, :]), 0.0)
        a_ref[0, h] = a
        b_ref[0, h] = b * (0.5 * inv_z)
        kap_ref[0, h] = jnp.exp(tau - c_top) * (0.5 * inv_z)
        return carry

    lax.fori_loop(0, PEER_HEADS, head_body, 0, unroll=ROUTE_HEAD_UNROLL)


def _peer_route(xb, wq_t, sk, tb):
    t = xb.shape[0]
    tb = min(tb, t)
    kern = functools.partial(_route_kernel, tb=tb)
    return pl.pallas_call(
        kern,
        grid=(t // tb,),
        in_specs=[pl.BlockSpec((tb, D_MODEL), lambda i: (i, 0)),
                  pl.BlockSpec((PEER_HEADS * PEER_KEY_DIM, D_MODEL), lambda i: (0, 0)),
                  pl.BlockSpec((PEER_HEADS, 2, N_KEYS, PEER_HALF), lambda i: (0, 0, 0, 0))],
        out_specs=[pl.BlockSpec((1, PEER_HEADS, N_KEYS, tb), lambda i: (i, 0, 0, 0)),
                   pl.BlockSpec((1, PEER_HEADS, N_KEYS, tb), lambda i: (i, 0, 0, 0)),
                   pl.BlockSpec((1, PEER_HEADS, 1, tb), lambda i: (i, 0, 0, 0))],
        out_shape=[jax.ShapeDtypeStruct((t // tb, PEER_HEADS, N_KEYS, tb), F32),
                   jax.ShapeDtypeStruct((t // tb, PEER_HEADS, N_KEYS, tb), F32),
                   jax.ShapeDtypeStruct((t // tb, PEER_HEADS, 1, tb), F32)],
        scratch_shapes=[pltpu.VMEM((PEER_HEADS * PEER_KEY_DIM, tb), F32),
                        pltpu.VMEM((PEER_TOPK, tb), F32),
                        pltpu.VMEM((PEER_TOPK, tb), F32),
                        pltpu.VMEM((3 * SUBLANES, tb), F32)],
        compiler_params=_cparams(("arbitrary",)),
        name="peer_route",
    )(xb, wq_t, sk)


def _peer_kernel(xb_ref, x_ref, u_ref, vt_ref, a_ref, b_ref, kap_ref,
                 g_ref, bt_ref, y_ref, acc, ht, gt, *, tb, eb, rtb):
    e = pl.program_id(1)
    ni = eb // N_KEYS
    nlt = tb // LANES
    inv_sqrt2 = 2.0 ** -0.5

    @pl.when(e == 0)
    def _():
        acc[...] = jnp.zeros_like(acc)

    cgw = min(MXU_COLS, tb)
    ncg = tb // cgw
    lpc = cgw // LANES

    def stage_a(half, slot, cg):
        c0 = cg * cgw
        ht[slot, :, c0:c0 + cgw] = lax.dot_general(
            u_ref[half * eb:(half + 1) * eb, :], xb_ref[c0:c0 + cgw, :],
            (((1,), (1,)), ((), ())), preferred_element_type=F32)

    def stage_b(slot, row0, cg):
        for ii in range(ni):
            for lt in range(cg * lpc, (cg + 1) * lpc):
                l0 = lt * LANES
                sb, r0 = divmod(l0, rtb)
                wsum = jnp.zeros((N_KEYS, LANES), F32)
                for h in range(PEER_HEADS):
                    p = (a_ref[sb, h, row0 + ii:row0 + ii + 1, r0:r0 + LANES]
                         * b_ref[sb, h, :, r0:r0 + LANES])
                    wsum = wsum + jnp.where(p > kap_ref[sb, h, :, r0:r0 + LANES], p, 0.0)
                hh = ht[slot, ii * N_KEYS:(ii + 1) * N_KEYS, l0:l0 + LANES]
                gval = wsum * (hh * (1.0 + lax.erf(hh * inv_sqrt2)))
                gt[slot, ii * N_KEYS:(ii + 1) * N_KEYS, l0:l0 + LANES] = gval.astype(BF16)

    def stage_c(slot, half, cg):
        c0 = cg * cgw
        for mc in range(D_MODEL // PEER_ROW_CHUNK):
            r0 = mc * PEER_ROW_CHUNK
            acc[r0:r0 + PEER_ROW_CHUNK, c0:c0 + cgw] += jnp.dot(
                vt_ref[0, r0:r0 + PEER_ROW_CHUNK, half * eb:(half + 1) * eb],
                gt[slot, :, c0:c0 + cgw], preferred_element_type=F32)

    for half in range(2):
        for cg in range(ncg):
            stage_a(half, half, cg)
    for half in range(2):
        for cg in range(ncg):
            stage_b(half, half * ni, cg)
            stage_c(half, half, cg)

    @pl.when(e == pl.num_programs(1) - 1)
    def _():
        for lt in range(nlt):
            l0 = lt * LANES
            z = ALPHA * x_ref[l0:l0 + LANES, :] + acc[:, l0:l0 + LANES].T
            y_ref[l0:l0 + LANES, :] = _layer_norm(z, g_ref[...], bt_ref[...])


def _peer_dense_ln(xb, x, u, vt, a, b, kap, g, bt, *, tb, eb):
    t = x.shape[0]
    tb = min(tb, t)
    assert 2 * eb == SUBLANES * N_KEYS
    n_e = N_EXPERTS // (2 * eb)
    rtb = a.shape[-1]
    nsb = tb // rtb
    kern = functools.partial(_peer_kernel, tb=tb, eb=eb, rtb=rtb)
    row = lambda i, e: (i, 0)
    fixed = lambda i, e: (0, 0)
    per_tok = lambda i, e: (i, 0, 0, 0)
    return pl.pallas_call(
        kern,
        grid=(t // tb, n_e),
        in_specs=[pl.BlockSpec((tb, D_MODEL), row),
                  pl.BlockSpec((tb, D_MODEL), row),
                  pl.BlockSpec((2 * eb, D_MODEL), lambda i, e: (e, 0)),
                  pl.BlockSpec((1, D_MODEL, 2 * eb), lambda i, e: (e, 0, 0)),
                  pl.BlockSpec((nsb, PEER_HEADS, SUBLANES, rtb), lambda i, e: (i, 0, e, 0)),
                  pl.BlockSpec((nsb, PEER_HEADS, N_KEYS, rtb), per_tok),
                  pl.BlockSpec((nsb, PEER_HEADS, 1, rtb), per_tok),
                  pl.BlockSpec((1, D_MODEL), fixed),
                  pl.BlockSpec((1, D_MODEL), fixed)],
        out_specs=pl.BlockSpec((tb, D_MODEL), row),
        out_shape=jax.ShapeDtypeStruct((t, D_MODEL), F32),
        scratch_shapes=[pltpu.VMEM((D_MODEL, tb), F32),
                        pltpu.VMEM((2, eb, tb), F32),
                        pltpu.VMEM((2, eb, tb), BF16)],
        compiler_params=_cparams(("arbitrary", "arbitrary")),
        name="peer_dense",
    )(xb, x, u, vt, a, b, kap, g, bt)


def _ple_kernel(x_ref, p_ref, wg_ref, wp_ref, y_ref):
    x = x_ref[...]
    gate = jax.nn.sigmoid(jnp.dot(x.astype(BF16), wg_ref[...], preferred_element_type=F32))
    emb = jnp.dot(p_ref[...].astype(BF16), wp_ref[...], preferred_element_type=F32)
    y_ref[...] = x + gate * emb


def _ple(x, p, wg, wp, tm):
    t = x.shape[0]
    tm = min(tm, t)
    row = lambda i: (i, 0)
    fixed = lambda i: (0, 0)
    return pl.pallas_call(
        _ple_kernel,
        grid=(t // tm,),
        in_specs=[pl.BlockSpec((tm, D_MODEL), row),
                  pl.BlockSpec((tm, PLE_DIM), row),
                  pl.BlockSpec((D_MODEL, D_MODEL), fixed),
                  pl.BlockSpec((PLE_DIM, D_MODEL), fixed)],
        out_specs=pl.BlockSpec((tm, D_MODEL), row),
        out_shape=jax.ShapeDtypeStruct((t, D_MODEL), F32),
        compiler_params=_cparams(("arbitrary",)),
        name="ple",
    )(x, p, wg, wp)


def _channel_mixer(x1, x1b, p, wts, *, tm, route_tb, peer_tb, peer_eb):
    a, b, kap = _peer_route(x1b, wts["wq_t"], wts["sk"], route_tb)
    x2 = _peer_dense_ln(x1b, x1, wts["u"], wts["vt"], a, b, kap, wts["g1"], wts["b1"],
                        tb=peer_tb, eb=peer_eb)
    return _ple(x2, p, wts["wg"], wts["wp"], tm)


def _attention_mixer(x, hist_k, hist_v, pos0, wts, *, tm, qb, sub):
    bsz, s, _ = x.shape
    xf = x.reshape(bsz * s, D_MODEL)
    qkv = _matmul(xf, wts["wqkv"], BF16, tm).reshape(bsz, s, 3 * D_MODEL)
    if hist_k is None:
        o = _band_attention(qkv, 0, qkv, 1, qkv, 1, qkv, 2, qkv, 2, wts["rel_bias"],
                            qb=qb, sub=sub, pos0=pos0, seq_len=s, prev_is_shifted=True)
    else:
        sp = -(-s // CHUNK) * CHUNK
        qkv_p = jnp.pad(qkv, ((0, 0), (0, sp - s), (0, 0)))
        o = _band_attention(qkv_p, 0, hist_k, 0, qkv_p, 1, hist_v, 0, qkv_p, 2, wts["rel_bias"],
                            qb=sp, sub=CHUNK, pos0=pos0, seq_len=s, prev_is_shifted=False)
        o = o[:, :s]
    x1, x1b = _proj_ln(o.reshape(bsz * s, D_MODEL), wts["wo"], xf, wts["g0"], wts["b0"], tm)
    return x1, x1b


def _new_kv(x_rows, wts, tm):
    return _matmul(x_rows, wts["wkv"], F32, tm)


def kernel(x_prompt, x_sample, cache_att_k, cache_att_v, state_pool, p_prompt, p_sample,
           att_w_qkv, att_w_o, att_rel_bias, pool_w_in, pool_w_grp, pool_scale,
           peer_w_query, peer_sub_keys, peer_u, peer_v, ln_g, ln_b, ple_w_proj, ple_w_gate):
    bp, sp_len, _ = x_prompt.shape
    bs, ss_len, _ = x_sample.shape
    tm = 512
    cfg_p = dict(tm=tm, route_tb=256, peer_tb=512, peer_eb=512)
    cfg_s = dict(tm=tm, route_tb=bs * ss_len, peer_tb=bs * ss_len, peer_eb=512)
    xp, xs = x_prompt, x_sample
    kp_out, vp_out, ks_out, vs_out, pp_out, ps_out = [], [], [], [], [], []
    for i in range(DEPTH):
        j = i // 2
        wts = {
            "wq_t": peer_w_query[i].T.astype(BF16),
            "sk": peer_sub_keys[i].astype(BF16),
            "u": peer_u[i].astype(BF16),
            "vt": peer_v[i].astype(BF16).reshape(-1, PEER_STEP_EXPERTS, D_MODEL).transpose(0, 2, 1),
            "g0": ln_g[i, 0].reshape(1, D_MODEL), "b0": ln_b[i, 0].reshape(1, D_MODEL),
            "g1": ln_g[i, 1].reshape(1, D_MODEL), "b1": ln_b[i, 1].reshape(1, D_MODEL),
            "wg": ple_w_gate[i].astype(BF16),
            "wp": ple_w_proj[i].astype(BF16),
        }
        if i % 2 == 0:
            wts["wqkv"] = att_w_qkv[j].astype(BF16)
            wts["wkv"] = wts["wqkv"][:, D_MODEL:]
            wts["wo"] = att_w_o[j].astype(BF16)
            wts["rel_bias"] = att_rel_bias[j]
            keep = min(BAND_PAST, sp_len)
            x1p, x1pb = _attention_mixer(xp, None, None, 0, wts, tm=tm, qb=BAND_PAST, sub=2 * CHUNK)
            kv_p = _new_kv(xp[:, sp_len - keep:].reshape(bp * keep, D_MODEL), wts, tm)
            kv_p = kv_p.reshape(bp, keep, 2, N_HEADS, HEAD_DIM)
            kp_out.append(kv_p[:, :, 0])
            vp_out.append(kv_p[:, :, 1])
            hk = cache_att_k[j].astype(BF16).reshape(bs, -1, D_MODEL)
            hv = cache_att_v[j].astype(BF16).reshape(bs, -1, D_MODEL)
            pad = BAND_PAST - hk.shape[1]
            hk = jnp.pad(hk, ((0, 0), (pad, 0), (0, 0)))
            hv = jnp.pad(hv, ((0, 0), (pad, 0), (0, 0)))
            x1s, x1sb = _attention_mixer(xs, hk, hv, PAST_LEN, wts, tm=tm, qb=CHUNK, sub=CHUNK)
            kv_s = _new_kv(xs.reshape(bs * ss_len, D_MODEL), wts, tm)
            kv_s = kv_s.reshape(bs, ss_len, 2, N_HEADS, HEAD_DIM)
            ks_out.append(kv_s[:, :, 0])
            vs_out.append(kv_s[:, :, 1])
        else:
            w_in = pool_w_in[j].astype(BF16)
            w_grp = pool_w_grp[j].astype(BF16)
            scale = pool_scale[j].reshape(1, D_MODEL)
            zero_hist = jnp.zeros((bp, POOL_CARRY, D_MODEL), F32)
            x1p, x1pb, st_p = _pool_mixer_ln(xp, zero_hist, w_in, w_grp, scale, wts["g0"], wts["b0"],
                                            tm=tm, pos0=0)
            hist_s = jnp.pad(state_pool[j], ((0, 0), (POOL_CARRY - POOL_HIST, 0), (0, 0)))
            x1s, x1sb, st_s = _pool_mixer_ln(xs, hist_s, w_in, w_grp, scale, wts["g0"], wts["b0"],
                                            tm=tm, pos0=PAST_LEN)
            x1p = x1p.reshape(bp * sp_len, D_MODEL)
            x1pb = x1pb.reshape(bp * sp_len, D_MODEL)
            x1s = x1s.reshape(bs * ss_len, D_MODEL)
            x1sb = x1sb.reshape(bs * ss_len, D_MODEL)
            pp_out.append(st_p[:, POOL_CARRY - POOL_HIST:])
            ps_out.append(st_s[:, POOL_CARRY - POOL_HIST:])
        xp = _channel_mixer(x1p, x1pb, p_prompt[i].reshape(bp * sp_len, PLE_DIM), wts,
                            **cfg_p).reshape(bp, sp_len, D_MODEL)
        xs = _channel_mixer(x1s, x1sb, p_sample[i].reshape(bs * ss_len, PLE_DIM), wts,
                            **cfg_s).reshape(bs, ss_len, D_MODEL)
    return (xp, xs, jnp.stack(kp_out), jnp.stack(vp_out), jnp.stack(ks_out), jnp.stack(vs_out),
            jnp.stack(pp_out), jnp.stack(ps_out))
```

```python
import functools

import jax
import jax.numpy as jnp
import numpy as np
from jax import lax
from jax.experimental import pallas as pl
from jax.experimental.pallas import tpu as pltpu

F32 = jnp.float32
BF16 = jnp.bfloat16

D_MODEL = 2048
DEPTH = 2
PAST_LEN = 4096
CHUNK = 64
N_HEADS = 16
HEAD_DIM = D_MODEL // N_HEADS
LEFT_CHUNKS = 8
BAND_PAST = LEFT_CHUNKS * CHUNK
BAND = BAND_PAST + CHUNK
REL_MAX = 256
REL_MIN = -(CHUNK - 1)
POOL_WINDOWS = (2, 4, 8, 16)
N_POOL_GROUPS = len(POOL_WINDOWS)
POOL_GROUP_DIM = D_MODEL // N_POOL_GROUPS
POOL_HIST = max(POOL_WINDOWS) - 1
POOL_CARRY = 16
PEER_HEADS = 8
PEER_KEY_DIM = 256
PEER_HALF = PEER_KEY_DIM // 2
N_KEYS = 128
N_EXPERTS = N_KEYS * N_KEYS
PEER_TOPK = 16
PLE_DIM = 256
PEER_ROW_CHUNK = 512
ROUTE_HEAD_UNROLL = 4
ATTN_HEAD_GROUP = 4
ALPHA = (2 * DEPTH) ** 0.25
LN_EPS = 1e-5
NEG_INF = -1e30

LANES = 128
SUBLANES = 8
MXU_COLS = 256
PEER_STEP_EXPERTS = SUBLANES * N_KEYS
VMEM_LIMIT_BYTES = 56 * 1024 * 1024


def _cparams(sem):
    return pltpu.CompilerParams(dimension_semantics=sem, vmem_limit_bytes=VMEM_LIMIT_BYTES)


def _layer_norm(z, g, b):
    mu = jnp.mean(z, axis=-1, keepdims=True)
    zc = z - mu
    var = jnp.mean(zc * zc, axis=-1, keepdims=True)
    return zc * lax.rsqrt(var + LN_EPS) * g + b


def _mm_kernel(x_ref, w_ref, o_ref):
    x = x_ref[...].astype(BF16)
    o_ref[...] = jnp.dot(x, w_ref[...], preferred_element_type=F32).astype(o_ref.dtype)


def _matmul(x, w, out_dtype, tm):
    t, k = x.shape
    n = w.shape[1]
    tn = min(n, D_MODEL)
    tm = min(tm, t)
    return pl.pallas_call(
        _mm_kernel,
        grid=(n // tn, t // tm),
        in_specs=[pl.BlockSpec((tm, k), lambda j, i: (i, 0)),
                  pl.BlockSpec((k, tn), lambda j, i: (0, j))],
        out_specs=pl.BlockSpec((tm, tn), lambda j, i: (i, j)),
        out_shape=jax.ShapeDtypeStruct((t, n), out_dtype),
        compiler_params=_cparams(("arbitrary", "arbitrary")),
        name="proj",
    )(x, w)


def _attn_kernel(q_ref, kp_ref, kc_ref, vp_ref, vc_ref, bias_ref, o_ref, kwin, vwin,
                 *, qb, sub, pos0, seq_len):
    i = pl.program_id(1)
    w = sub + BAND_PAST
    kwin[0:BAND_PAST, :] = kp_ref[0]
    kwin[BAND_PAST:BAND_PAST + qb, :] = kc_ref[0]
    vwin[0:BAND_PAST, :] = vp_ref[0]
    vwin[BAND_PAST:BAND_PAST + qb, :] = vc_ref[0]
    scale = HEAD_DIM ** -0.5
    col = lax.broadcasted_iota(jnp.int32, (sub, w), 1)

    def sub_body(s, carry):
        r0 = pl.multiple_of(s * sub, sub)
        kpos = (pos0 - BAND_PAST) + i * qb + r0 + col
        valid = (kpos >= 0) & (kpos < pos0 + seq_len)
        for g0 in range(0, N_HEADS, ATTN_HEAD_GROUP):
            scs = []
            for h in range(g0, g0 + ATTN_HEAD_GROUP):
                c0 = h * HEAD_DIM
                q = q_ref[0, pl.ds(r0, sub), c0:c0 + HEAD_DIM]
                k = kwin[pl.ds(r0, w), c0:c0 + HEAD_DIM]
                scs.append(lax.dot_general(q, k, (((1,), (1,)), ((), ())),
                                           preferred_element_type=F32))
            sc = jnp.stack(scs, axis=0) * scale + bias_ref[g0:g0 + ATTN_HEAD_GROUP]
            sc = jnp.where(valid[None], sc, NEG_INF)
            m = jnp.max(sc, axis=-1, keepdims=True)
            p = jnp.exp(sc - m)
            inv_l = 1.0 / jnp.sum(p, axis=-1, keepdims=True)
            pb = p.astype(BF16)
            for gi in range(ATTN_HEAD_GROUP):
                c0 = (g0 + gi) * HEAD_DIM
                v = vwin[pl.ds(r0, w), c0:c0 + HEAD_DIM]
                o = jnp.dot(pb[gi], v, preferred_element_type=F32) * inv_l[gi]
                o_ref[0, pl.ds(r0, sub), c0:c0 + HEAD_DIM] = o.astype(o_ref.dtype)
        return carry

    lax.fori_loop(0, qb // sub, sub_body, 0)


def _band_bias(rel_bias, sub):
    n_rel = REL_MAX - REL_MIN + 1
    rev = rel_bias.astype(F32)[:, ::-1]
    lead = BAND_PAST - REL_MAX - REL_MIN
    ext = jnp.concatenate([jnp.broadcast_to(rev[:, :1], (N_HEADS, lead)), rev], axis=1)
    assert ext.shape[1] == CHUNK - 1 + BAND and lead + n_rel == ext.shape[1]
    bias = jnp.stack([ext[:, CHUNK - 1 - iq:CHUNK - 1 - iq + BAND] for iq in range(CHUNK)],
                     axis=1)
    nq = sub // CHUNK
    w = sub + BAND_PAST
    rows = []
    for m in range(nq):
        left = jnp.full((N_HEADS, CHUNK, m * CHUNK), NEG_INF, F32)
        right = jnp.full((N_HEADS, CHUNK, w - BAND - m * CHUNK), NEG_INF, F32)
        rows.append(jnp.concatenate([left, bias, right], axis=2))
    return jnp.concatenate(rows, axis=1)


def _band_attention(q_arr, q_col, kprev, kp_col, kcur, kc_col, vprev, vp_col, vcur, vc_col,
                    rel_bias, *, qb, sub, pos0, seq_len, prev_is_shifted):
    b, sq = q_arr.shape[0], q_arr.shape[1]
    bias = _band_bias(rel_bias, sub)
    w = sub + BAND_PAST
    if prev_is_shifted:
        assert qb == BAND_PAST
        prev_map = lambda col: (lambda bi, i: (bi, jnp.maximum(i - 1, 0), col))
    else:
        prev_map = lambda col: (lambda bi, i: (bi, 0, col))
    cur_map = lambda col: (lambda bi, i: (bi, i, col))
    kern = functools.partial(_attn_kernel, qb=qb, sub=sub, pos0=pos0, seq_len=seq_len)
    return pl.pallas_call(
        kern,
        grid=(b, sq // qb),
        in_specs=[pl.BlockSpec((1, qb, D_MODEL), cur_map(q_col)),
                  pl.BlockSpec((1, BAND_PAST, D_MODEL), prev_map(kp_col)),
                  pl.BlockSpec((1, qb, D_MODEL), cur_map(kc_col)),
                  pl.BlockSpec((1, BAND_PAST, D_MODEL), prev_map(vp_col)),
                  pl.BlockSpec((1, qb, D_MODEL), cur_map(vc_col)),
                  pl.BlockSpec((N_HEADS, sub, w), lambda bi, i: (0, 0, 0))],
        out_specs=pl.BlockSpec((1, qb, D_MODEL), lambda bi, i: (bi, i, 0)),
        out_shape=jax.ShapeDtypeStruct((b, sq, D_MODEL), BF16),
        scratch_shapes=[pltpu.VMEM((BAND_PAST + qb, D_MODEL), BF16),
                        pltpu.VMEM((BAND_PAST + qb, D_MODEL), BF16)],
        compiler_params=_cparams(("arbitrary", "arbitrary")),
        name="band_attn",
    )(q_arr, kprev, kcur, vprev, vcur, bias)


def _proj_ln_kernel(o_ref, w_ref, x_ref, g_ref, b_ref, y_ref, yb_ref):
    y = jnp.dot(o_ref[...], w_ref[...], preferred_element_type=F32)
    z = _layer_norm(ALPHA * x_ref[...] + y, g_ref[...], b_ref[...])
    y_ref[...] = z
    yb_ref[...] = z.astype(BF16)


def _proj_ln(o, w, x, g, b, tm):
    t = x.shape[0]
    tm = min(tm, t)
    row = lambda i: (i, 0)
    fixed = lambda i: (0, 0)
    return pl.pallas_call(
        _proj_ln_kernel,
        grid=(t // tm,),
        in_specs=[pl.BlockSpec((tm, D_MODEL), row),
                  pl.BlockSpec((D_MODEL, D_MODEL), fixed),
                  pl.BlockSpec((tm, D_MODEL), row),
                  pl.BlockSpec((1, D_MODEL), fixed),
                  pl.BlockSpec((1, D_MODEL), fixed)],
        out_specs=[pl.BlockSpec((tm, D_MODEL), row), pl.BlockSpec((tm, D_MODEL), row)],
        out_shape=[jax.ShapeDtypeStruct((t, D_MODEL), F32),
                   jax.ShapeDtypeStruct((t, D_MODEL), BF16)],
        compiler_params=_cparams(("arbitrary",)),
        name="proj_ln",
    )(o, w, x, g, b)


def _pool_kernel(x_ref, hist_ref, win_ref, wg_ref, sc_ref, g_ref, b_ref,
                 y_ref, yb_ref, st_ref, ext, *, tm, pos0):
    i = pl.program_id(1)

    @pl.when(i == 0)
    def _():
        ext[0:POOL_CARRY, :] = hist_ref[0]

    x = x_ref[0]
    u = jnp.dot(x.astype(BF16), win_ref[...], preferred_element_type=F32)
    ext[POOL_CARRY:POOL_CARRY + tm, :] = u
    row = lax.broadcasted_iota(jnp.int32, (tm, POOL_GROUP_DIM), 0)
    pos = (pos0 + i * tm + row + 1).astype(F32)
    zs = []
    for gi, wlen in enumerate(POOL_WINDOWS):
        c0 = gi * POOL_GROUP_DIM
        acc = u[:, c0:c0 + POOL_GROUP_DIM]
        for sft in range(1, wlen):
            acc = acc + ext[POOL_CARRY - sft:POOL_CARRY - sft + tm, c0:c0 + POOL_GROUP_DIM]
        mean = acc / jnp.minimum(float(wlen), pos)
        yg = (mean - u[:, c0:c0 + POOL_GROUP_DIM]).astype(BF16)
        zs.append(jnp.dot(yg, wg_ref[gi], preferred_element_type=F32))
    z = jnp.concatenate(zs, axis=1) * sc_ref[...]
    out = _layer_norm(ALPHA * x + z, g_ref[...], b_ref[...])
    y_ref[0] = out
    yb_ref[0] = out.astype(BF16)
    tail = ext[tm:tm + POOL_CARRY, :]
    st_ref[0] = tail
    ext[0:POOL_CARRY, :] = tail


def _pool_mixer_ln(x, hist, w_in, w_grp, scale, g, b, *, tm, pos0):
    bsz, s, _ = x.shape
    tm = min(tm, s)
    kern = functools.partial(_pool_kernel, tm=tm, pos0=pos0)
    blk = lambda bi, i: (bi, i, 0)
    per_b = lambda bi, i: (bi, 0, 0)
    fixed2 = lambda bi, i: (0, 0)
    return pl.pallas_call(
        kern,
        grid=(bsz, s // tm),
        in_specs=[pl.BlockSpec((1, tm, D_MODEL), blk),
                  pl.BlockSpec((1, POOL_CARRY, D_MODEL), per_b),
                  pl.BlockSpec((D_MODEL, D_MODEL), fixed2),
                  pl.BlockSpec((N_POOL_GROUPS, POOL_GROUP_DIM, POOL_GROUP_DIM),
                               lambda bi, i: (0, 0, 0)),
                  pl.BlockSpec((1, D_MODEL), fixed2),
                  pl.BlockSpec((1, D_MODEL), fixed2),
                  pl.BlockSpec((1, D_MODEL), fixed2)],
        out_specs=[pl.BlockSpec((1, tm, D_MODEL), blk),
                   pl.BlockSpec((1, tm, D_MODEL), blk),
                   pl.BlockSpec((1, POOL_CARRY, D_MODEL), per_b)],
        out_shape=[jax.ShapeDtypeStruct((bsz, s, D_MODEL), F32),
                   jax.ShapeDtypeStruct((bsz, s, D_MODEL), BF16),
                   jax.ShapeDtypeStruct((bsz, POOL_CARRY, D_MODEL), F32)],
        scratch_shapes=[pltpu.VMEM((POOL_CARRY + tm, D_MODEL), F32)],
        compiler_params=_cparams(("arbitrary", "arbitrary")),
        name="pool_mixer",
    )(x, hist, w_in, w_grp, scale, g, b)


def _extract_top(vals, n, dst_ref):
    for it in range(n):
        m = jnp.max(vals, axis=0, keepdims=True)
        dst_ref[it:it + 1, :] = m
        if it + 1 < n:
            vals = jnp.where(vals == m, -jnp.inf, vals)


def _route_kernel(x_ref, wq_ref, sk_ref, a_ref, b_ref, kap_ref, qt, r1, r2, cl, *, tb):
    qt[...] = lax.dot_general(wq_ref[...], x_ref[...], (((1,), (1,)), ((), ())),
                              preferred_element_type=F32)

    def head_body(h, carry):
        q1 = qt[pl.ds(pl.multiple_of(h * PEER_KEY_DIM, PEER_KEY_DIM), PEER_HALF), :]
        q2 = qt[pl.ds(pl.multiple_of(h * PEER_KEY_DIM + PEER_HALF, PEER_HALF), PEER_HALF), :]
        s1 = jnp.dot(sk_ref[h, 0], q1.astype(BF16), preferred_element_type=F32)
        s2 = jnp.dot(sk_ref[h, 1], q2.astype(BF16), preferred_element_type=F32)
        _extract_top(s1, PEER_TOPK, r1)
        _extract_top(s2, PEER_TOPK, r2)
        lo2 = r2[0:SUBLANES, :]
        tiles = [r1[a:a + 1, :] + lo2 for a in range(SUBLANES)]
        tiles.append(r1[0:1, :] + r2[SUBLANES:PEER_TOPK, :])
        tiles.append(r1[SUBLANES:PEER_TOPK, :] + r2[0:1, :])
        cand = jnp.concatenate(tiles, axis=0)
        _extract_top(cand, PEER_TOPK + 1, cl)
        c_top = cl[0:1, :]
        tau = 0.5 * (cl[PEER_TOPK - 1:PEER_TOPK, :] + cl[PEER_TOPK:PEER_TOPK + 1, :])
        zsum = jnp.sum(jnp.where(cand > tau, jnp.exp(cand - c_top), 0.0), axis=0, keepdims=True)
        inv_z = 1.0 / zsum
        a = jnp.where(s1 >= r1[PEER_TOPK - 1:PEER_TOPK, :], jnp.exp(s1 - r1[0:1, :]), 0.0)
        b = jnp.where(s2 >= r2[PEER_TOPK - 1:PEER_TOPK, :], jnp.exp(s2 - r2[0:1, :]), 0.0)
        a_ref[0, h] = a
        b_ref[0, h] = b * (0.5 * inv_z)
        kap_ref[0, h] = jnp.exp(tau - c_top) * (0.5 * inv_z)
        return carry

    lax.fori_loop(0, PEER_HEADS, head_body, 0, unroll=ROUTE_HEAD_UNROLL)


def _peer_route(xb, wq_t, sk, tb):
    t = xb.shape[0]
    tb = min(tb, t)
    kern = functools.partial(_route_kernel, tb=tb)
    return pl.pallas_call(
        kern,
        grid=(t // tb,),
        in_specs=[pl.BlockSpec((tb, D_MODEL), lambda i: (i, 0)),
                  pl.BlockSpec((PEER_HEADS * PEER_KEY_DIM, D_MODEL), lambda i: (0, 0)),
                  pl.BlockSpec((PEER_HEADS, 2, N_KEYS, PEER_HALF), lambda i: (0, 0, 0, 0))],
        out_specs=[pl.BlockSpec((1, PEER_HEADS, N_KEYS, tb), lambda i: (i, 0, 0, 0)),
                   pl.BlockSpec((1, PEER_HEADS, N_KEYS, tb), lambda i: (i, 0, 0, 0)),
                   pl.BlockSpec((1, PEER_HEADS, 1, tb), lambda i: (i, 0, 0, 0))],
        out_shape=[jax.ShapeDtypeStruct((t // tb, PEER_HEADS, N_KEYS, tb), F32),
                   jax.ShapeDtypeStruct((t // tb, PEER_HEADS, N_KEYS, tb), F32),
                   jax.ShapeDtypeStruct((t // tb, PEER_HEADS, 1, tb), F32)],
        scratch_shapes=[pltpu.VMEM((PEER_HEADS * PEER_KEY_DIM, tb), F32),
                        pltpu.VMEM((PEER_TOPK, tb), F32),
                        pltpu.VMEM((PEER_TOPK, tb), F32),
                        pltpu.VMEM((3 * SUBLANES, tb), F32)],
        compiler_params=_cparams(("arbitrary",)),
        name="peer_route",
    )(xb, wq_t, sk)


def _peer_kernel(x_ref, u_ref, vt_ref, a_ref, b_ref, kap_ref,
                 g_ref, bt_ref, y_ref, acc, ht, gt, xt, *, tb, eb, rtb):
    e = pl.program_id(1)
    ni = eb // N_KEYS
    nlt = tb // LANES
    inv_sqrt2 = 2.0 ** -0.5

    @pl.when(e == 0)
    def _():
        acc[...] = jnp.zeros_like(acc)
        for lt in range(nlt):
            l0 = lt * LANES
            xt[:, l0:l0 + LANES] = x_ref[l0:l0 + LANES, :].T.astype(BF16)

    cgw = min(MXU_COLS, tb)
    ncg = tb // cgw
    lpc = cgw // LANES

    def stage_a(half, slot, cg):
        c0 = cg * cgw
        ht[slot, :, c0:c0 + cgw] = jnp.dot(
            u_ref[half * eb:(half + 1) * eb, :], xt[:, c0:c0 + cgw],
            preferred_element_type=F32)

    def stage_b(slot, row0, cg):
        for ii in range(ni):
            for lt in range(cg * lpc, (cg + 1) * lpc):
                l0 = lt * LANES
                sb, r0 = divmod(l0, rtb)
                wsum = jnp.zeros((N_KEYS, LANES), F32)
                for h in range(PEER_HEADS):
                    p = (a_ref[sb, h, row0 + ii:row0 + ii + 1, r0:r0 + LANES]
                         * b_ref[sb, h, :, r0:r0 + LANES])
                    wsum = wsum + jnp.where(p > kap_ref[sb, h, :, r0:r0 + LANES], p, 0.0)
                hh = ht[slot, ii * N_KEYS:(ii + 1) * N_KEYS, l0:l0 + LANES]
                gval = wsum * (hh * (1.0 + lax.erf(hh * inv_sqrt2)))
                gt[slot, ii * N_KEYS:(ii + 1) * N_KEYS, l0:l0 + LANES] = gval.astype(BF16)

    def stage_c(slot, half, cg):
        c0 = cg * cgw
        for mc in range(D_MODEL // PEER_ROW_CHUNK):
            r0 = mc * PEER_ROW_CHUNK
            acc[r0:r0 + PEER_ROW_CHUNK, c0:c0 + cgw] += jnp.dot(
                vt_ref[0, r0:r0 + PEER_ROW_CHUNK, half * eb:(half + 1) * eb],
                gt[slot, :, c0:c0 + cgw], preferred_element_type=F32)

    for half in range(2):
        for cg in range(ncg):
            stage_a(half, half, cg)
    for half in range(2):
        for cg in range(ncg):
            stage_b(half, half * ni, cg)
            stage_c(half, half, cg)

    @pl.when(e == pl.num_programs(1) - 1)
    def _():
        for lt in range(nlt):
            l0 = lt * LANES
            z = ALPHA * x_ref[l0:l0 + LANES, :] + acc[:, l0:l0 + LANES].T
            y_ref[l0:l0 + LANES, :] = _layer_norm(z, g_ref[...], bt_ref[...])


def _peer_dense_ln(x, u, vt, a, b, kap, g, bt, *, tb, eb):
    t = x.shape[0]
    tb = min(tb, t)
    assert 2 * eb == SUBLANES * N_KEYS
    n_e = N_EXPERTS // (2 * eb)
    rtb = a.shape[-1]
    nsb = tb // rtb
    kern = functools.partial(_peer_kernel, tb=tb, eb=eb, rtb=rtb)
    row = lambda i, e: (i, 0)
    fixed = lambda i, e: (0, 0)
    per_tok = lambda i, e: (i, 0, 0, 0)
    return pl.pallas_call(
        kern,
        grid=(t // tb, n_e),
        in_specs=[pl.BlockSpec((tb, D_MODEL), row),
                  pl.BlockSpec((2 * eb, D_MODEL), lambda i, e: (e, 0)),
                  pl.BlockSpec((1, D_MODEL, 2 * eb), lambda i, e: (e, 0, 0)),
                  pl.BlockSpec((nsb, PEER_HEADS, SUBLANES, rtb), lambda i, e: (i, 0, e, 0)),
                  pl.BlockSpec((nsb, PEER_HEADS, N_KEYS, rtb), per_tok),
                  pl.BlockSpec((nsb, PEER_HEADS, 1, rtb), per_tok),
                  pl.BlockSpec((1, D_MODEL), fixed),
                  pl.BlockSpec((1, D_MODEL), fixed)],
        out_specs=pl.BlockSpec((tb, D_MODEL), row),
        out_shape=jax.ShapeDtypeStruct((t, D_MODEL), F32),
        scratch_shapes=[pltpu.VMEM((D_MODEL, tb), F32),
                        pltpu.VMEM((2, eb, tb), F32),
                        pltpu.VMEM((2, eb, tb), BF16),
                        pltpu.VMEM((D_MODEL, tb), BF16)],
        compiler_params=_cparams(("arbitrary", "arbitrary")),
        name="peer_dense",
    )(x, u, vt, a, b, kap, g, bt)


def _ple_kernel(x_ref, p_ref, wg_ref, wp_ref, y_ref):
    x = x_ref[...]
    gate = jax.nn.sigmoid(jnp.dot(x.astype(BF16), wg_ref[...], preferred_element_type=F32))
    emb = jnp.dot(p_ref[...].astype(BF16), wp_ref[...], preferred_element_type=F32)
    y_ref[...] = x + gate * emb


def _ple(x, p, wg, wp, tm):
    t = x.shape[0]
    tm = min(tm, t)
    row = lambda i: (i, 0)
    fixed = lambda i: (0, 0)
    return pl.pallas_call(
        _ple_kernel,
        grid=(t // tm,),
        in_specs=[pl.BlockSpec((tm, D_MODEL), row),
                  pl.BlockSpec((tm, PLE_DIM), row),
                  pl.BlockSpec((D_MODEL, D_MODEL), fixed),
                  pl.BlockSpec((PLE_DIM, D_MODEL), fixed)],
        out_specs=pl.BlockSpec((tm, D_MODEL), row),
        out_shape=jax.ShapeDtypeStruct((t, D_MODEL), F32),
        compiler_params=_cparams(("arbitrary",)),
        name="ple",
    )(x, p, wg, wp)


def _channel_mixer(x1, x1b, p, wts, *, tm, route_tb, peer_tb, peer_eb):
    a, b, kap = _peer_route(x1b, wts["wq_t"], wts["sk"], route_tb)
    x2 = _peer_dense_ln(x1, wts["u"], wts["vt"], a, b, kap, wts["g1"], wts["b1"],
                        tb=peer_tb, eb=peer_eb)
    return _ple(x2, p, wts["wg"], wts["wp"], tm)


def _attention_mixer(x, hist_k, hist_v, pos0, wts, *, tm, qb, sub):
    bsz, s, _ = x.shape
    xf = x.reshape(bsz * s, D_MODEL)
    qkv = _matmul(xf, wts["wqkv"], BF16, tm).reshape(bsz, s, 3 * D_MODEL)
    if hist_k is None:
        o = _band_attention(qkv, 0, qkv, 1, qkv, 1, qkv, 2, qkv, 2, wts["rel_bias"],
                            qb=qb, sub=sub, pos0=pos0, seq_len=s, prev_is_shifted=True)
    else:
        sp = -(-s // CHUNK) * CHUNK
        qkv_p = jnp.pad(qkv, ((0, 0), (0, sp - s), (0, 0)))
        o = _band_attention(qkv_p, 0, hist_k, 0, qkv_p, 1, hist_v, 0, qkv_p, 2, wts["rel_bias"],
                            qb=sp, sub=CHUNK, pos0=pos0, seq_len=s, prev_is_shifted=False)
        o = o[:, :s]
    x1, x1b = _proj_ln(o.reshape(bsz * s, D_MODEL), wts["wo"], xf, wts["g0"], wts["b0"], tm)
    return x1, x1b


def _new_kv(x_rows, wts, tm):
    return _matmul(x_rows, wts["wkv"], F32, tm)


def kernel(x_prompt, x_sample, cache_att_k, cache_att_v, state_pool, p_prompt, p_sample,
           att_w_qkv, att_w_o, att_rel_bias, pool_w_in, pool_w_grp, pool_scale,
           peer_w_query, peer_sub_keys, peer_u, peer_v, ln_g, ln_b, ple_w_proj, ple_w_gate):
    bp, sp_len, _ = x_prompt.shape
    bs, ss_len, _ = x_sample.shape
    tm = 512
    cfg_p = dict(tm=tm, route_tb=256, peer_tb=512, peer_eb=512)
    cfg_s = dict(tm=tm, route_tb=bs * ss_len, peer_tb=bs * ss_len, peer_eb=512)
    xp, xs = x_prompt, x_sample
    kp_out, vp_out, ks_out, vs_out, pp_out, ps_out = [], [], [], [], [], []
    for i in range(DEPTH):
        j = i // 2
        wts = {
            "wq_t": peer_w_query[i].T.astype(BF16),
            "sk": peer_sub_keys[i].astype(BF16),
            "u": peer_u[i].astype(BF16),
            "vt": peer_v[i].astype(BF16).reshape(-1, PEER_STEP_EXPERTS, D_MODEL).transpose(0, 2, 1),
            "g0": ln_g[i, 0].reshape(1, D_MODEL), "b0": ln_b[i, 0].reshape(1, D_MODEL),
            "g1": ln_g[i, 1].reshape(1, D_MODEL), "b1": ln_b[i, 1].reshape(1, D_MODEL),
            "wg": ple_w_gate[i].astype(BF16),
            "wp": ple_w_proj[i].astype(BF16),
        }
        if i % 2 == 0:
            wts["wqkv"] = att_w_qkv[j].astype(BF16)
            wts["wkv"] = wts["wqkv"][:, D_MODEL:]
            wts["wo"] = att_w_o[j].astype(BF16)
            wts["rel_bias"] = att_rel_bias[j]
            keep = min(BAND_PAST, sp_len)
            x1p, x1pb = _attention_mixer(xp, None, None, 0, wts, tm=tm, qb=BAND_PAST, sub=2 * CHUNK)
            kv_p = _new_kv(xp[:, sp_len - keep:].reshape(bp * keep, D_MODEL), wts, tm)
            kv_p = kv_p.reshape(bp, keep, 2, N_HEADS, HEAD_DIM)
            kp_out.append(kv_p[:, :, 0])
            vp_out.append(kv_p[:, :, 1])
            hk = cache_att_k[j].astype(BF16).reshape(bs, -1, D_MODEL)
            hv = cache_att_v[j].astype(BF16).reshape(bs, -1, D_MODEL)
            pad = BAND_PAST - hk.shape[1]
            hk = jnp.pad(hk, ((0, 0), (pad, 0), (0, 0)))
            hv = jnp.pad(hv, ((0, 0), (pad, 0), (0, 0)))
            x1s, x1sb = _attention_mixer(xs, hk, hv, PAST_LEN, wts, tm=tm, qb=CHUNK, sub=CHUNK)
            kv_s = _new_kv(xs.reshape(bs * ss_len, D_MODEL), wts, tm)
            kv_s = kv_s.reshape(bs, ss_len, 2, N_HEADS, HEAD_DIM)
            ks_out.append(kv_s[:, :, 0])
            vs_out.append(kv_s[:, :, 1])
        else:
            w_in = pool_w_in[j].astype(BF16)
            w_grp = pool_w_grp[j].astype(BF16)
            scale = pool_scale[j].reshape(1, D_MODEL)
            zero_hist = jnp.zeros((bp, POOL_CARRY, D_MODEL), F32)
            x1p, x1pb, st_p = _pool_mixer_ln(xp, zero_hist, w_in, w_grp, scale, wts["g0"], wts["b0"],
                                            tm=tm, pos0=0)
            hist_s = jnp.pad(state_pool[j], ((0, 0), (POOL_CARRY - POOL_HIST, 0), (0, 0)))
            x1s, x1sb, st_s = _pool_mixer_ln(xs, hist_s, w_in, w_grp, scale, wts["g0"], wts["b0"],
                                            tm=tm, pos0=PAST_LEN)
            x1p = x1p.reshape(bp * sp_len, D_MODEL)
            x1pb = x1pb.reshape(bp * sp_len, D_MODEL)
            x1s = x1s.reshape(bs * ss_len, D_MODEL)
            x1sb = x1sb.reshape(bs * ss_len, D_MODEL)
            pp_out.append(st_p[:, POOL_CARRY - POOL_HIST:])
            ps_out.append(st_s[:, POOL_CARRY - POOL_HIST:])
        xp = _channel_mixer(x1p, x1pb, p_prompt[i].reshape(bp * sp_len, PLE_DIM), wts,
                            **cfg_p).reshape(bp, sp_len, D_MODEL)
        xs = _channel_mixer(x1s, x1sb, p_sample[i].reshape(bs * ss_len, PLE_DIM), wts,
                            **cfg_s).reshape(bs, ss_len, D_MODEL)
    return (xp, xs, jnp.stack(kp_out), jnp.stack(vp_out), jnp.stack(ks_out), jnp.stack(vs_out),
            jnp.stack(pp_out), jnp.stack(ps_out))
```

```python
import functools

import jax
import jax.numpy as jnp
import numpy as np
from jax import lax
from jax.experimental import pallas as pl
from jax.experimental.pallas import tpu as pltpu

F32 = jnp.float32
BF16 = jnp.bfloat16

D_MODEL = 2048
DEPTH = 2
PAST_LEN = 4096
CHUNK = 64
N_HEADS = 16
HEAD_DIM = D_MODEL // N_HEADS
LEFT_CHUNKS = 8
BAND_PAST = LEFT_CHUNKS * CHUNK
BAND = BAND_PAST + CHUNK
REL_MAX = 256
REL_MIN = -(CHUNK - 1)
POOL_WINDOWS = (2, 4, 8, 16)
N_POOL_GROUPS = len(POOL_WINDOWS)
POOL_GROUP_DIM = D_MODEL // N_POOL_GROUPS
POOL_HIST = max(POOL_WINDOWS) - 1
POOL_CARRY = 16
PEER_HEADS = 8
PEER_KEY_DIM = 256
PEER_HALF = PEER_KEY_DIM // 2
N_KEYS = 128
N_EXPERTS = N_KEYS * N_KEYS
PEER_TOPK = 16
PLE_DIM = 256
PEER_ROW_CHUNK = 512
ROUTE_HEAD_UNROLL = 4
ATTN_HEAD_GROUP = 4
ALPHA = (2 * DEPTH) ** 0.25
LN_EPS = 1e-5
NEG_INF = -1e30

LANES = 128
SUBLANES = 8
MXU_COLS = 256
PEER_STEP_EXPERTS = SUBLANES * N_KEYS
VMEM_LIMIT_BYTES = 56 * 1024 * 1024
ROW_BLOCK = 512


def _cparams(sem):
    return pltpu.CompilerParams(dimension_semantics=sem, vmem_limit_bytes=VMEM_LIMIT_BYTES)


def _layer_norm(z, g, b):
    mu = jnp.mean(z, axis=-1, keepdims=True)
    zc = z - mu
    var = jnp.mean(zc * zc, axis=-1, keepdims=True)
    return zc * lax.rsqrt(var + LN_EPS) * g + b


def _mm_kernel(x_ref, w_ref, o_ref):
    x = x_ref[...].astype(BF16)
    o_ref[...] = jnp.dot(x, w_ref[...], preferred_element_type=F32).astype(o_ref.dtype)


def _matmul(x, w, out_dtype, tm):
    t, k = x.shape
    n = w.shape[1]
    tn = min(n, D_MODEL)
    tm = min(tm, t)
    return pl.pallas_call(
        _mm_kernel,
        grid=(n // tn, t // tm),
        in_specs=[pl.BlockSpec((tm, k), lambda j, i: (i, 0)),
                  pl.BlockSpec((k, tn), lambda j, i: (0, j))],
        out_specs=pl.BlockSpec((tm, tn), lambda j, i: (i, j)),
        out_shape=jax.ShapeDtypeStruct((t, n), out_dtype),
        compiler_params=_cparams(("arbitrary", "arbitrary")),
        name="proj",
    )(x, w)


def _attn_kernel(q_ref, kp_ref, kc_ref, vp_ref, vc_ref, bias_ref, o_ref, kwin, vwin,
                 *, qb, sub, pos0, seq_len):
    i = pl.program_id(1)
    w = sub + BAND_PAST
    kwin[0:BAND_PAST, :] = kp_ref[0]
    kwin[BAND_PAST:BAND_PAST + qb, :] = kc_ref[0]
    vwin[0:BAND_PAST, :] = vp_ref[0]
    vwin[BAND_PAST:BAND_PAST + qb, :] = vc_ref[0]
    scale = HEAD_DIM ** -0.5
    col = lax.broadcasted_iota(jnp.int32, (sub, w), 1)

    def sub_body(s, carry):
        r0 = pl.multiple_of(s * sub, sub)
        kpos = (pos0 - BAND_PAST) + i * qb + r0 + col
        valid = (kpos >= 0) & (kpos < pos0 + seq_len)
        for g0 in range(0, N_HEADS, ATTN_HEAD_GROUP):
            scs = []
            for h in range(g0, g0 + ATTN_HEAD_GROUP):
                c0 = h * HEAD_DIM
                q = q_ref[0, pl.ds(r0, sub), c0:c0 + HEAD_DIM]
                k = kwin[pl.ds(r0, w), c0:c0 + HEAD_DIM]
                scs.append(lax.dot_general(q, k, (((1,), (1,)), ((), ())),
                                           preferred_element_type=F32))
            sc = jnp.stack(scs, axis=0) * scale + bias_ref[g0:g0 + ATTN_HEAD_GROUP]
            sc = jnp.where(valid[None], sc, NEG_INF)
            m = jnp.max(sc, axis=-1, keepdims=True)
            p = jnp.exp(sc - m)
            inv_l = 1.0 / jnp.sum(p, axis=-1, keepdims=True)
            pb = p.astype(BF16)
            for gi in range(ATTN_HEAD_GROUP):
                c0 = (g0 + gi) * HEAD_DIM
                v = vwin[pl.ds(r0, w), c0:c0 + HEAD_DIM]
                o = jnp.dot(pb[gi], v, preferred_element_type=F32) * inv_l[gi]
                o_ref[0, pl.ds(r0, sub), c0:c0 + HEAD_DIM] = o.astype(o_ref.dtype)
        return carry

    lax.fori_loop(0, qb // sub, sub_body, 0)


def _band_bias(rel_bias, sub):
    n_rel = REL_MAX - REL_MIN + 1
    rev = rel_bias.astype(F32)[:, ::-1]
    lead = BAND_PAST - REL_MAX - REL_MIN
    ext = jnp.concatenate([jnp.broadcast_to(rev[:, :1], (N_HEADS, lead)), rev], axis=1)
    assert ext.shape[1] == CHUNK - 1 + BAND and lead + n_rel == ext.shape[1]
    bias = jnp.stack([ext[:, CHUNK - 1 - iq:CHUNK - 1 - iq + BAND] for iq in range(CHUNK)],
                     axis=1)
    nq = sub // CHUNK
    w = sub + BAND_PAST
    rows = []
    for m in range(nq):
        left = jnp.full((N_HEADS, CHUNK, m * CHUNK), NEG_INF, F32)
        right = jnp.full((N_HEADS, CHUNK, w - BAND - m * CHUNK), NEG_INF, F32)
        rows.append(jnp.concatenate([left, bias, right], axis=2))
    return jnp.concatenate(rows, axis=1)


def _band_attention(q_arr, q_col, kprev, kp_col, kcur, kc_col, vprev, vp_col, vcur, vc_col,
                    rel_bias, *, qb, sub, pos0, seq_len, prev_is_shifted):
    b, sq = q_arr.shape[0], q_arr.shape[1]
    bias = _band_bias(rel_bias, sub)
    w = sub + BAND_PAST
    if prev_is_shifted:
        assert qb == BAND_PAST
        prev_map = lambda col: (lambda bi, i: (bi, jnp.maximum(i - 1, 0), col))
    else:
        prev_map = lambda col: (lambda bi, i: (bi, 0, col))
    cur_map = lambda col: (lambda bi, i: (bi, i, col))
    kern = functools.partial(_attn_kernel, qb=qb, sub=sub, pos0=pos0, seq_len=seq_len)
    return pl.pallas_call(
        kern,
        grid=(b, sq // qb),
        in_specs=[pl.BlockSpec((1, qb, D_MODEL), cur_map(q_col)),
                  pl.BlockSpec((1, BAND_PAST, D_MODEL), prev_map(kp_col)),
                  pl.BlockSpec((1, qb, D_MODEL), cur_map(kc_col)),
                  pl.BlockSpec((1, BAND_PAST, D_MODEL), prev_map(vp_col)),
                  pl.BlockSpec((1, qb, D_MODEL), cur_map(vc_col)),
                  pl.BlockSpec((N_HEADS, sub, w), lambda bi, i: (0, 0, 0))],
        out_specs=pl.BlockSpec((1, qb, D_MODEL), lambda bi, i: (bi, i, 0)),
        out_shape=jax.ShapeDtypeStruct((b, sq, D_MODEL), BF16),
        scratch_shapes=[pltpu.VMEM((BAND_PAST + qb, D_MODEL), BF16),
                        pltpu.VMEM((BAND_PAST + qb, D_MODEL), BF16)],
        compiler_params=_cparams(("arbitrary", "arbitrary")),
        name="band_attn",
    )(q_arr, kprev, kcur, vprev, vcur, bias)


def _proj_ln_kernel(o_ref, w_ref, x_ref, g_ref, b_ref, y_ref, yb_ref):
    y = jnp.dot(o_ref[...], w_ref[...], preferred_element_type=F32)
    z = _layer_norm(ALPHA * x_ref[...] + y, g_ref[...], b_ref[...])
    y_ref[...] = z
    yb_ref[...] = z.astype(BF16)


def _proj_ln(o, w, x, g, b, tm):
    t = x.shape[0]
    tm = min(tm, t)
    row = lambda i: (i, 0)
    fixed = lambda i: (0, 0)
    return pl.pallas_call(
        _proj_ln_kernel,
        grid=(t // tm,),
        in_specs=[pl.BlockSpec((tm, D_MODEL), row),
                  pl.BlockSpec((D_MODEL, D_MODEL), fixed),
                  pl.BlockSpec((tm, D_MODEL), row),
                  pl.BlockSpec((1, D_MODEL), fixed),
                  pl.BlockSpec((1, D_MODEL), fixed)],
        out_specs=[pl.BlockSpec((tm, D_MODEL), row), pl.BlockSpec((tm, D_MODEL), row)],
        out_shape=[jax.ShapeDtypeStruct((t, D_MODEL), F32),
                   jax.ShapeDtypeStruct((t, D_MODEL), BF16)],
        compiler_params=_cparams(("arbitrary",)),
        name="proj_ln",
    )(o, w, x, g, b)


def _pool_kernel(x_ref, hist_ref, win_ref, wg_ref, sc_ref, g_ref, b_ref,
                 y_ref, yb_ref, st_ref, ext, *, tm, pos0):
    i = pl.program_id(1)

    @pl.when(i == 0)
    def _():
        ext[0:POOL_CARRY, :] = hist_ref[0]

    x = x_ref[0]
    u = jnp.dot(x.astype(BF16), win_ref[...], preferred_element_type=F32)
    ext[POOL_CARRY:POOL_CARRY + tm, :] = u
    row = lax.broadcasted_iota(jnp.int32, (tm, POOL_GROUP_DIM), 0)
    pos = (pos0 + i * tm + row + 1).astype(F32)
    zs = []
    for gi, wlen in enumerate(POOL_WINDOWS):
        c0 = gi * POOL_GROUP_DIM
        acc = u[:, c0:c0 + POOL_GROUP_DIM]
        for sft in range(1, wlen):
            acc = acc + ext[POOL_CARRY - sft:POOL_CARRY - sft + tm, c0:c0 + POOL_GROUP_DIM]
        mean = acc / jnp.minimum(float(wlen), pos)
        yg = (mean - u[:, c0:c0 + POOL_GROUP_DIM]).astype(BF16)
        zs.append(jnp.dot(yg, wg_ref[gi], preferred_element_type=F32))
    z = jnp.concatenate(zs, axis=1) * sc_ref[...]
    out = _layer_norm(ALPHA * x + z, g_ref[...], b_ref[...])
    y_ref[0] = out
    yb_ref[0] = out.astype(BF16)
    tail = ext[tm:tm + POOL_CARRY, :]
    st_ref[0] = tail
    ext[0:POOL_CARRY, :] = tail


def _pool_mixer_ln(x, hist, w_in, w_grp, scale, g, b, *, tm, pos0):
    bsz, s, _ = x.shape
    tm = min(tm, s)
    kern = functools.partial(_pool_kernel, tm=tm, pos0=pos0)
    blk = lambda bi, i: (bi, i, 0)
    per_b = lambda bi, i: (bi, 0, 0)
    fixed2 = lambda bi, i: (0, 0)
    return pl.pallas_call(
        kern,
        grid=(bsz, s // tm),
        in_specs=[pl.BlockSpec((1, tm, D_MODEL), blk),
                  pl.BlockSpec((1, POOL_CARRY, D_MODEL), per_b),
                  pl.BlockSpec((D_MODEL, D_MODEL), fixed2),
                  pl.BlockSpec((N_POOL_GROUPS, POOL_GROUP_DIM, POOL_GROUP_DIM),
                               lambda bi, i: (0, 0, 0)),
                  pl.BlockSpec((1, D_MODEL), fixed2),
                  pl.BlockSpec((1, D_MODEL), fixed2),
                  pl.BlockSpec((1, D_MODEL), fixed2)],
        out_specs=[pl.BlockSpec((1, tm, D_MODEL), blk),
                   pl.BlockSpec((1, tm, D_MODEL), blk),
                   pl.BlockSpec((1, POOL_CARRY, D_MODEL), per_b)],
        out_shape=[jax.ShapeDtypeStruct((bsz, s, D_MODEL), F32),
                   jax.ShapeDtypeStruct((bsz, s, D_MODEL), BF16),
                   jax.ShapeDtypeStruct((bsz, POOL_CARRY, D_MODEL), F32)],
        scratch_shapes=[pltpu.VMEM((POOL_CARRY + tm, D_MODEL), F32)],
        compiler_params=_cparams(("arbitrary", "arbitrary")),
        name="pool_mixer",
    )(x, hist, w_in, w_grp, scale, g, b)


def _extract_top(vals, n, dst_ref):
    for it in range(n):
        m = jnp.max(vals, axis=0, keepdims=True)
        dst_ref[it:it + 1, :] = m
        if it + 1 < n:
            vals = jnp.where(vals == m, -jnp.inf, vals)


def _route_kernel(x_ref, wq_ref, sk_ref, a_ref, b_ref, kap_ref, qt, r1, r2, cl, *, tb):
    qt[...] = lax.dot_general(wq_ref[...], x_ref[...], (((1,), (1,)), ((), ())),
                              preferred_element_type=F32)

    def head_body(h, carry):
        q1 = qt[pl.ds(pl.multiple_of(h * PEER_KEY_DIM, PEER_KEY_DIM), PEER_HALF), :]
        q2 = qt[pl.ds(pl.multiple_of(h * PEER_KEY_DIM + PEER_HALF, PEER_HALF), PEER_HALF), :]
        s1 = jnp.dot(sk_ref[h, 0], q1.astype(BF16), preferred_element_type=F32)
        s2 = jnp.dot(sk_ref[h, 1], q2.astype(BF16), preferred_element_type=F32)
        _extract_top(s1, PEER_TOPK, r1)
        _extract_top(s2, PEER_TOPK, r2)
        lo2 = r2[0:SUBLANES, :]
        tiles = [r1[a:a + 1, :] + lo2 for a in range(SUBLANES)]
        tiles.append(r1[0:1, :] + r2[SUBLANES:PEER_TOPK, :])
        tiles.append(r1[SUBLANES:PEER_TOPK, :] + r2[0:1, :])
        cand = jnp.concatenate(tiles, axis=0)
        _extract_top(cand, PEER_TOPK + 1, cl)
        c_top = cl[0:1, :]
        tau = 0.5 * (cl[PEER_TOPK - 1:PEER_TOPK, :] + cl[PEER_TOPK:PEER_TOPK + 1, :])
        zsum = jnp.sum(jnp.where(cand > tau, jnp.exp(cand - c_top), 0.0), axis=0, keepdims=True)
        inv_z = 1.0 / zsum
        a = jnp.where(s1 >= r1[PEER_TOPK - 1:PEER_TOPK, :], jnp.exp(s1 - r1[0:1, :]), 0.0)
        b = jnp.where(s2 >= r2[PEER_TOPK - 1:PEER_TOPK, :], jnp.exp(s2 - r2[0:1, :]), 0.0)
        a_ref[0, h] = a
        b_ref[0, h] = b * (0.5 * inv_z)
        kap_ref[0, h] = jnp.exp(tau - c_top) * (0.5 * inv_z)
        return carry

    lax.fori_loop(0, PEER_HEADS, head_body, 0, unroll=ROUTE_HEAD_UNROLL)


def _peer_route(xb, wq_t, sk, tb):
    t = xb.shape[0]
    tb = min(tb, t)
    kern = functools.partial(_route_kernel, tb=tb)
    return pl.pallas_call(
        kern,
        grid=(t // tb,),
        in_specs=[pl.BlockSpec((tb, D_MODEL), lambda i: (i, 0)),
                  pl.BlockSpec((PEER_HEADS * PEER_KEY_DIM, D_MODEL), lambda i: (0, 0)),
                  pl.BlockSpec((PEER_HEADS, 2, N_KEYS, PEER_HALF), lambda i: (0, 0, 0, 0))],
        out_specs=[pl.BlockSpec((1, PEER_HEADS, N_KEYS, tb), lambda i: (i, 0, 0, 0)),
                   pl.BlockSpec((1, PEER_HEADS, N_KEYS, tb), lambda i: (i, 0, 0, 0)),
                   pl.BlockSpec((1, PEER_HEADS, 1, tb), lambda i: (i, 0, 0, 0))],
        out_shape=[jax.ShapeDtypeStruct((t // tb, PEER_HEADS, N_KEYS, tb), F32),
                   jax.ShapeDtypeStruct((t // tb, PEER_HEADS, N_KEYS, tb), F32),
                   jax.ShapeDtypeStruct((t // tb, PEER_HEADS, 1, tb), F32)],
        scratch_shapes=[pltpu.VMEM((PEER_HEADS * PEER_KEY_DIM, tb), F32),
                        pltpu.VMEM((PEER_TOPK, tb), F32),
                        pltpu.VMEM((PEER_TOPK, tb), F32),
                        pltpu.VMEM((3 * SUBLANES, tb), F32)],
        compiler_params=_cparams(("arbitrary",)),
        name="peer_route",
    )(xb, wq_t, sk)


def _peer_kernel(x_ref, u_ref, vt_ref, a_ref, b_ref, kap_ref,
                 g_ref, bt_ref, y_ref, acc, ht, gt, xt, *, tb, eb, rtb):
    e = pl.program_id(1)
    ni = eb // N_KEYS
    nlt = tb // LANES
    inv_sqrt2 = 2.0 ** -0.5

    @pl.when(e == 0)
    def _():
        acc[...] = jnp.zeros_like(acc)
        for lt in range(nlt):
            l0 = lt * LANES
            xt[:, l0:l0 + LANES] = x_ref[l0:l0 + LANES, :].T.astype(BF16)

    cgw = min(MXU_COLS, tb)
    ncg = tb // cgw
    lpc = cgw // LANES

    def stage_a(half, slot, cg):
        c0 = cg * cgw
        ht[slot, :, c0:c0 + cgw] = jnp.dot(
            u_ref[half * eb:(half + 1) * eb, :], xt[:, c0:c0 + cgw],
            preferred_element_type=F32)

    def stage_b(slot, row0, cg):
        for ii in range(ni):
            for lt in range(cg * lpc, (cg + 1) * lpc):
                l0 = lt * LANES
                sb, r0 = divmod(l0, rtb)
                wsum = jnp.zeros((N_KEYS, LANES), F32)
                for h in range(PEER_HEADS):
                    p = (a_ref[sb, h, row0 + ii:row0 + ii + 1, r0:r0 + LANES]
                         * b_ref[sb, h, :, r0:r0 + LANES])
                    wsum = wsum + jnp.where(p > kap_ref[sb, h, :, r0:r0 + LANES], p, 0.0)
                hh = ht[slot, ii * N_KEYS:(ii + 1) * N_KEYS, l0:l0 + LANES]
                gval = wsum * (hh * (1.0 + lax.erf(hh * inv_sqrt2)))
                g0 = slot * eb + ii * N_KEYS
                gt[g0:g0 + N_KEYS, l0:l0 + LANES] = gval.astype(BF16)

    def stage_c(cg):
        c0 = cg * cgw
        for mc in range(D_MODEL // PEER_ROW_CHUNK):
            r0 = mc * PEER_ROW_CHUNK
            acc[r0:r0 + PEER_ROW_CHUNK, c0:c0 + cgw] += jnp.dot(
                vt_ref[0, r0:r0 + PEER_ROW_CHUNK, :], gt[:, c0:c0 + cgw],
                preferred_element_type=F32)

    for half in range(2):
        for cg in range(ncg):
            stage_a(half, half, cg)
    for cg in range(ncg):
        for half in range(2):
            stage_b(half, half * ni, cg)
        stage_c(cg)

    @pl.when(e == pl.num_programs(1) - 1)
    def _():
        for lt in range(nlt):
            l0 = lt * LANES
            z = ALPHA * x_ref[l0:l0 + LANES, :] + acc[:, l0:l0 + LANES].T
            y_ref[l0:l0 + LANES, :] = _layer_norm(z, g_ref[...], bt_ref[...])


def _peer_dense_ln(x, u, vt, a, b, kap, g, bt, *, tb, eb):
    t = x.shape[0]
    tb = min(tb, t)
    assert 2 * eb == SUBLANES * N_KEYS
    n_e = N_EXPERTS // (2 * eb)
    rtb = a.shape[-1]
    nsb = tb // rtb
    kern = functools.partial(_peer_kernel, tb=tb, eb=eb, rtb=rtb)
    row = lambda i, e: (i, 0)
    fixed = lambda i, e: (0, 0)
    per_tok = lambda i, e: (i, 0, 0, 0)
    return pl.pallas_call(
        kern,
        grid=(t // tb, n_e),
        in_specs=[pl.BlockSpec((tb, D_MODEL), row),
                  pl.BlockSpec((2 * eb, D_MODEL), lambda i, e: (e, 0)),
                  pl.BlockSpec((1, D_MODEL, 2 * eb), lambda i, e: (e, 0, 0)),
                  pl.BlockSpec((nsb, PEER_HEADS, SUBLANES, rtb), lambda i, e: (i, 0, e, 0)),
                  pl.BlockSpec((nsb, PEER_HEADS, N_KEYS, rtb), per_tok),
                  pl.BlockSpec((nsb, PEER_HEADS, 1, rtb), per_tok),
                  pl.BlockSpec((1, D_MODEL), fixed),
                  pl.BlockSpec((1, D_MODEL), fixed)],
        out_specs=pl.BlockSpec((tb, D_MODEL), row),
        out_shape=jax.ShapeDtypeStruct((t, D_MODEL), F32),
        scratch_shapes=[pltpu.VMEM((D_MODEL, tb), F32),
                        pltpu.VMEM((2, eb, tb), F32),
                        pltpu.VMEM((2 * eb, tb), BF16),
                        pltpu.VMEM((D_MODEL, tb), BF16)],
        compiler_params=_cparams(("arbitrary", "arbitrary")),
        name="peer_dense",
    )(x, u, vt, a, b, kap, g, bt)


def _ple_kernel(x_ref, p_ref, wg_ref, wp_ref, y_ref):
    x = x_ref[...]
    gate = jax.nn.sigmoid(jnp.dot(x.astype(BF16), wg_ref[...], preferred_element_type=F32))
    emb = jnp.dot(p_ref[...].astype(BF16), wp_ref[...], preferred_element_type=F32)
    y_ref[...] = x + gate * emb


def _ple(x, p, wg, wp, tm):
    t = x.shape[0]
    tm = min(tm, t)
    row = lambda i: (i, 0)
    fixed = lambda i: (0, 0)
    return pl.pallas_call(
        _ple_kernel,
        grid=(t // tm,),
        in_specs=[pl.BlockSpec((tm, D_MODEL), row),
                  pl.BlockSpec((tm, PLE_DIM), row),
                  pl.BlockSpec((D_MODEL, D_MODEL), fixed),
                  pl.BlockSpec((PLE_DIM, D_MODEL), fixed)],
        out_specs=pl.BlockSpec((tm, D_MODEL), row),
        out_shape=jax.ShapeDtypeStruct((t, D_MODEL), F32),
        compiler_params=_cparams(("arbitrary",)),
        name="ple",
    )(x, p, wg, wp)


def _channel_mixer_tiles(n_tokens):
    return dict(
        tm=ROW_BLOCK,
        route_tb=min(n_tokens, 2 * LANES),
        peer_tb=min(n_tokens, ROW_BLOCK),
        peer_eb=PEER_STEP_EXPERTS // 2,
    )


def _channel_mixer(x1, x1b, p, wts, *, tm, route_tb, peer_tb, peer_eb):
    a, b, kap = _peer_route(x1b, wts["wq_t"], wts["sk"], route_tb)
    x2 = _peer_dense_ln(x1, wts["u"], wts["vt"], a, b, kap, wts["g1"], wts["b1"],
                        tb=peer_tb, eb=peer_eb)
    return _ple(x2, p, wts["wg"], wts["wp"], tm)


def _attention_mixer(x, hist_k, hist_v, pos0, wts, *, tm, qb, sub):
    bsz, s, _ = x.shape
    xf = x.reshape(bsz * s, D_MODEL)
    qkv = _matmul(xf, wts["wqkv"], BF16, tm).reshape(bsz, s, 3 * D_MODEL)
    if hist_k is None:
        o = _band_attention(qkv, 0, qkv, 1, qkv, 1, qkv, 2, qkv, 2, wts["rel_bias"],
                            qb=qb, sub=sub, pos0=pos0, seq_len=s, prev_is_shifted=True)
    else:
        sp = -(-s // CHUNK) * CHUNK
        qkv_p = jnp.pad(qkv, ((0, 0), (0, sp - s), (0, 0)))
        o = _band_attention(qkv_p, 0, hist_k, 0, qkv_p, 1, hist_v, 0, qkv_p, 2, wts["rel_bias"],
                            qb=sp, sub=CHUNK, pos0=pos0, seq_len=s, prev_is_shifted=False)
        o = o[:, :s]
    x1, x1b = _proj_ln(o.reshape(bsz * s, D_MODEL), wts["wo"], xf, wts["g0"], wts["b0"], tm)
    return x1, x1b


def _new_kv(x_rows, wts, tm):
    return _matmul(x_rows, wts["wkv"], F32, tm)


def kernel(x_prompt, x_sample, cache_att_k, cache_att_v, state_pool, p_prompt, p_sample,
           att_w_qkv, att_w_o, att_rel_bias, pool_w_in, pool_w_grp, pool_scale,
           peer_w_query, peer_sub_keys, peer_u, peer_v, ln_g, ln_b, ple_w_proj, ple_w_gate):
    bp, sp_len, _ = x_prompt.shape
    bs, ss_len, _ = x_sample.shape
    tm = ROW_BLOCK
    cfg_p = _channel_mixer_tiles(bp * sp_len)
    cfg_s = _channel_mixer_tiles(bs * ss_len)
    xp, xs = x_prompt, x_sample
    kp_out, vp_out, ks_out, vs_out, pp_out, ps_out = [], [], [], [], [], []
    for i in range(DEPTH):
        j = i // 2
        wts = {
            "wq_t": peer_w_query[i].T.astype(BF16),
            "sk": peer_sub_keys[i].astype(BF16),
            "u": peer_u[i].astype(BF16),
            "vt": peer_v[i].astype(BF16).reshape(-1, PEER_STEP_EXPERTS, D_MODEL).transpose(0, 2, 1),
            "g0": ln_g[i, 0].reshape(1, D_MODEL), "b0": ln_b[i, 0].reshape(1, D_MODEL),
            "g1": ln_g[i, 1].reshape(1, D_MODEL), "b1": ln_b[i, 1].reshape(1, D_MODEL),
            "wg": ple_w_gate[i].astype(BF16),
            "wp": ple_w_proj[i].astype(BF16),
        }
        if i % 2 == 0:
            wts["wqkv"] = att_w_qkv[j].astype(BF16)
            wts["wkv"] = wts["wqkv"][:, D_MODEL:]
            wts["wo"] = att_w_o[j].astype(BF16)
            wts["rel_bias"] = att_rel_bias[j]
            keep = min(BAND_PAST, sp_len)
            x1p, x1pb = _attention_mixer(xp, None, None, 0, wts, tm=tm, qb=BAND_PAST, sub=2 * CHUNK)
            kv_p = _new_kv(xp[:, sp_len - keep:].reshape(bp * keep, D_MODEL), wts, tm)
            kv_p = kv_p.reshape(bp, keep, 2, N_HEADS, HEAD_DIM)
            kp_out.append(kv_p[:, :, 0])
            vp_out.append(kv_p[:, :, 1])
            hk = cache_att_k[j].astype(BF16).reshape(bs, -1, D_MODEL)
            hv = cache_att_v[j].astype(BF16).reshape(bs, -1, D_MODEL)
            pad = BAND_PAST - hk.shape[1]
            hk = jnp.pad(hk, ((0, 0), (pad, 0), (0, 0)))
            hv = jnp.pad(hv, ((0, 0), (pad, 0), (0, 0)))
            x1s, x1sb = _attention_mixer(xs, hk, hv, PAST_LEN, wts, tm=tm, qb=CHUNK, sub=CHUNK)
            kv_s = _new_kv(xs.reshape(bs * ss_len, D_MODEL), wts, tm)
            kv_s = kv_s.reshape(bs, ss_len, 2, N_HEADS, HEAD_DIM)
            ks_out.append(kv_s[:, :, 0])
            vs_out.append(kv_s[:, :, 1])
        else:
            w_in = pool_w_in[j].astype(BF16)
            w_grp = pool_w_grp[j].astype(BF16)
            scale = pool_scale[j].reshape(1, D_MODEL)
            zero_hist = jnp.zeros((bp, POOL_CARRY, D_MODEL), F32)
            x1p, x1pb, st_p = _pool_mixer_ln(xp, zero_hist, w_in, w_grp, scale, wts["g0"], wts["b0"],
                                            tm=tm, pos0=0)
            hist_s = jnp.pad(state_pool[j], ((0, 0), (POOL_CARRY - POOL_HIST, 0), (0, 0)))
            x1s, x1sb, st_s = _pool_mixer_ln(xs, hist_s, w_in, w_grp, scale, wts["g0"], wts["b0"],
                                            tm=tm, pos0=PAST_LEN)
            x1p = x1p.reshape(bp * sp_len, D_MODEL)
            x1pb = x1pb.reshape(bp * sp_len, D_MODEL)
            x1s = x1s.reshape(bs * ss_len, D_MODEL)
            x1sb = x1sb.reshape(bs * ss_len, D_MODEL)
            pp_out.append(st_p[:, POOL_CARRY - POOL_HIST:])
            ps_out.append(st_s[:, POOL_CARRY - POOL_HIST:])
        xp = _channel_mixer(x1p, x1pb, p_prompt[i].reshape(bp * sp_len, PLE_DIM), wts,
                            **cfg_p).reshape(bp, sp_len, D_MODEL)
        xs = _channel_mixer(x1s, x1sb, p_sample[i].reshape(bs * ss_len, PLE_DIM), wts,
                            **cfg_s).reshape(bs, ss_len, D_MODEL)
    return (xp, xs, jnp.stack(kp_out), jnp.stack(vp_out), jnp.stack(ks_out), jnp.stack(vs_out),
            jnp.stack(pp_out), jnp.stack(ps_out))
```

```python
import functools

import jax
import jax.numpy as jnp
import numpy as np
from jax import lax
from jax.experimental import pallas as pl
from jax.experimental.pallas import tpu as pltpu

F32 = jnp.float32
BF16 = jnp.bfloat16

D_MODEL = 2048
DEPTH = 2
PAST_LEN = 4096
CHUNK = 64
N_HEADS = 16
HEAD_DIM = D_MODEL // N_HEADS
LEFT_CHUNKS = 8
BAND_PAST = LEFT_CHUNKS * CHUNK
BAND = BAND_PAST + CHUNK
REL_MAX = 256
REL_MIN = -(CHUNK - 1)
POOL_WINDOWS = (2, 4, 8, 16)
N_POOL_GROUPS = len(POOL_WINDOWS)
POOL_GROUP_DIM = D_MODEL // N_POOL_GROUPS
POOL_HIST = max(POOL_WINDOWS) - 1
POOL_CARRY = 16
PEER_HEADS = 8
PEER_KEY_DIM = 256
PEER_HALF = PEER_KEY_DIM // 2
N_KEYS = 128
N_EXPERTS = N_KEYS * N_KEYS
PEER_TOPK = 16
PLE_DIM = 256
PEER_ROW_CHUNK = 512
ROUTE_HEAD_UNROLL = 4
ATTN_HEAD_GROUP = 4
ALPHA = (2 * DEPTH) ** 0.25
LN_EPS = 1e-5
NEG_INF = -1e30

LANES = 128
SUBLANES = 8
MXU_COLS = 256
PEER_STEP_EXPERTS = SUBLANES * N_KEYS
VMEM_LIMIT_BYTES = 56 * 1024 * 1024
ROW_BLOCK = 512


def _cparams(sem):
    return pltpu.CompilerParams(dimension_semantics=sem, vmem_limit_bytes=VMEM_LIMIT_BYTES)


def _layer_norm(z, g, b):
    mu = jnp.mean(z, axis=-1, keepdims=True)
    zc = z - mu
    var = jnp.mean(zc * zc, axis=-1, keepdims=True)
    return zc * lax.rsqrt(var + LN_EPS) * g + b


def _mm_kernel(x_ref, w_ref, o_ref):
    x = x_ref[...].astype(BF16)
    o_ref[...] = jnp.dot(x, w_ref[...], preferred_element_type=F32).astype(o_ref.dtype)


def _matmul(x, w, out_dtype, tm):
    t, k = x.shape
    n = w.shape[1]
    tn = min(n, D_MODEL)
    tm = min(tm, t)
    return pl.pallas_call(
        _mm_kernel,
        grid=(n // tn, t // tm),
        in_specs=[pl.BlockSpec((tm, k), lambda j, i: (i, 0)),
                  pl.BlockSpec((k, tn), lambda j, i: (0, j))],
        out_specs=pl.BlockSpec((tm, tn), lambda j, i: (i, j)),
        out_shape=jax.ShapeDtypeStruct((t, n), out_dtype),
        compiler_params=_cparams(("arbitrary", "arbitrary")),
        name="proj",
    )(x, w)


def _qkv_kernel(x_ref, w_ref, q_ref, k_ref, v_ref):
    x = x_ref[...].astype(BF16)
    for n, o_ref in enumerate((q_ref, k_ref, v_ref)):
        o_ref[...] = jnp.dot(x, w_ref[:, n * D_MODEL:(n + 1) * D_MODEL],
                             preferred_element_type=F32).astype(o_ref.dtype)


def _qkv_proj(x, w, tm):
    t = x.shape[0]
    tm = min(tm, t)
    row = lambda i: (i, 0)
    return pl.pallas_call(
        _qkv_kernel,
        grid=(t // tm,),
        in_specs=[pl.BlockSpec((tm, D_MODEL), row),
                  pl.BlockSpec((D_MODEL, 3 * D_MODEL), lambda i: (0, 0),
                               pipeline_mode=pl.Buffered(1))],
        out_specs=[pl.BlockSpec((tm, D_MODEL), row)] * 3,
        out_shape=[jax.ShapeDtypeStruct((t, D_MODEL), BF16)] * 3,
        compiler_params=_cparams(("arbitrary",)),
        name="qkv_proj",
    )(x, w)


def _attn_kernel(q_ref, kp_ref, kc_ref, vp_ref, vc_ref, bias_ref, o_ref, kwin, vwin,
                 *, qb, sub, pos0, seq_len):
    i = pl.program_id(1)
    w = sub + BAND_PAST
    kwin[0:BAND_PAST, :] = kp_ref[0]
    kwin[BAND_PAST:BAND_PAST + qb, :] = kc_ref[0]
    vwin[0:BAND_PAST, :] = vp_ref[0]
    vwin[BAND_PAST:BAND_PAST + qb, :] = vc_ref[0]
    scale = HEAD_DIM ** -0.5
    col = lax.broadcasted_iota(jnp.int32, (sub, w), 1)

    def sub_body(s, carry):
        r0 = pl.multiple_of(s * sub, sub)
        kpos = (pos0 - BAND_PAST) + i * qb + r0 + col
        valid = (kpos >= 0) & (kpos < pos0 + seq_len)
        for g0 in range(0, N_HEADS, ATTN_HEAD_GROUP):
            scs = []
            for h in range(g0, g0 + ATTN_HEAD_GROUP):
                c0 = h * HEAD_DIM
                q = q_ref[0, pl.ds(r0, sub), c0:c0 + HEAD_DIM]
                k = kwin[pl.ds(r0, w), c0:c0 + HEAD_DIM]
                scs.append(lax.dot_general(q, k, (((1,), (1,)), ((), ())),
                                           preferred_element_type=F32))
            sc = jnp.stack(scs, axis=0) * scale + bias_ref[g0:g0 + ATTN_HEAD_GROUP]
            sc = jnp.where(valid[None], sc, NEG_INF)
            m = jnp.max(sc, axis=-1, keepdims=True)
            p = jnp.exp(sc - m)
            inv_l = 1.0 / jnp.sum(p, axis=-1, keepdims=True)
            pb = p.astype(BF16)
            for gi in range(ATTN_HEAD_GROUP):
                c0 = (g0 + gi) * HEAD_DIM
                v = vwin[pl.ds(r0, w), c0:c0 + HEAD_DIM]
                o = jnp.dot(pb[gi], v, preferred_element_type=F32) * inv_l[gi]
                o_ref[0, pl.ds(r0, sub), c0:c0 + HEAD_DIM] = o.astype(o_ref.dtype)
        return carry

    lax.fori_loop(0, qb // sub, sub_body, 0)


def _band_bias(rel_bias, sub):
    n_rel = REL_MAX - REL_MIN + 1
    rev = rel_bias.astype(F32)[:, ::-1]
    lead = BAND_PAST - REL_MAX - REL_MIN
    ext = jnp.concatenate([jnp.broadcast_to(rev[:, :1], (N_HEADS, lead)), rev], axis=1)
    assert ext.shape[1] == CHUNK - 1 + BAND and lead + n_rel == ext.shape[1]
    bias = jnp.stack([ext[:, CHUNK - 1 - iq:CHUNK - 1 - iq + BAND] for iq in range(CHUNK)],
                     axis=1)
    nq = sub // CHUNK
    w = sub + BAND_PAST
    rows = []
    for m in range(nq):
        left = jnp.full((N_HEADS, CHUNK, m * CHUNK), NEG_INF, F32)
        right = jnp.full((N_HEADS, CHUNK, w - BAND - m * CHUNK), NEG_INF, F32)
        rows.append(jnp.concatenate([left, bias, right], axis=2))
    return jnp.concatenate(rows, axis=1)


def _band_attention(q_arr, q_col, kprev, kp_col, kcur, kc_col, vprev, vp_col, vcur, vc_col,
                    rel_bias, *, qb, sub, pos0, seq_len, prev_is_shifted):
    b, sq = q_arr.shape[0], q_arr.shape[1]
    bias = _band_bias(rel_bias, sub)
    w = sub + BAND_PAST
    if prev_is_shifted:
        assert qb == BAND_PAST
        prev_map = lambda col: (lambda bi, i: (bi, jnp.maximum(i - 1, 0), col))
    else:
        prev_map = lambda col: (lambda bi, i: (bi, 0, col))
    cur_map = lambda col: (lambda bi, i: (bi, i, col))
    kern = functools.partial(_attn_kernel, qb=qb, sub=sub, pos0=pos0, seq_len=seq_len)
    return pl.pallas_call(
        kern,
        grid=(b, sq // qb),
        in_specs=[pl.BlockSpec((1, qb, D_MODEL), cur_map(q_col)),
                  pl.BlockSpec((1, BAND_PAST, D_MODEL), prev_map(kp_col)),
                  pl.BlockSpec((1, qb, D_MODEL), cur_map(kc_col)),
                  pl.BlockSpec((1, BAND_PAST, D_MODEL), prev_map(vp_col)),
                  pl.BlockSpec((1, qb, D_MODEL), cur_map(vc_col)),
                  pl.BlockSpec((N_HEADS, sub, w), lambda bi, i: (0, 0, 0))],
        out_specs=pl.BlockSpec((1, qb, D_MODEL), lambda bi, i: (bi, i, 0)),
        out_shape=jax.ShapeDtypeStruct((b, sq, D_MODEL), BF16),
        scratch_shapes=[pltpu.VMEM((BAND_PAST + qb, D_MODEL), BF16),
                        pltpu.VMEM((BAND_PAST + qb, D_MODEL), BF16)],
        compiler_params=_cparams(("arbitrary", "arbitrary")),
        name="band_attn",
    )(q_arr, kprev, kcur, vprev, vcur, bias)


def _proj_ln_kernel(o_ref, w_ref, x_ref, g_ref, b_ref, y_ref, yb_ref):
    y = jnp.dot(o_ref[...], w_ref[...], preferred_element_type=F32)
    z = _layer_norm(ALPHA * x_ref[...] + y, g_ref[...], b_ref[...])
    y_ref[...] = z
    yb_ref[...] = z.astype(BF16)


def _proj_ln(o, w, x, g, b, tm):
    t = x.shape[0]
    tm = min(tm, t)
    row = lambda i: (i, 0)
    fixed = lambda i: (0, 0)
    return pl.pallas_call(
        _proj_ln_kernel,
        grid=(t // tm,),
        in_specs=[pl.BlockSpec((tm, D_MODEL), row),
                  pl.BlockSpec((D_MODEL, D_MODEL), fixed),
                  pl.BlockSpec((tm, D_MODEL), row),
                  pl.BlockSpec((1, D_MODEL), fixed),
                  pl.BlockSpec((1, D_MODEL), fixed)],
        out_specs=[pl.BlockSpec((tm, D_MODEL), row), pl.BlockSpec((tm, D_MODEL), row)],
        out_shape=[jax.ShapeDtypeStruct((t, D_MODEL), F32),
                   jax.ShapeDtypeStruct((t, D_MODEL), BF16)],
        compiler_params=_cparams(("arbitrary",)),
        name="proj_ln",
    )(o, w, x, g, b)


def _pool_kernel(x_ref, hist_ref, win_ref, wg_ref, sc_ref, g_ref, b_ref,
                 y_ref, yb_ref, st_ref, ext, *, tm, pos0):
    i = pl.program_id(1)

    @pl.when(i == 0)
    def _():
        ext[0:POOL_CARRY, :] = hist_ref[0]

    x = x_ref[0]
    u = jnp.dot(x.astype(BF16), win_ref[...], preferred_element_type=F32)
    ext[POOL_CARRY:POOL_CARRY + tm, :] = u
    row = lax.broadcasted_iota(jnp.int32, (tm, POOL_GROUP_DIM), 0)
    pos = (pos0 + i * tm + row + 1).astype(F32)
    zs = []
    for gi, wlen in enumerate(POOL_WINDOWS):
        c0 = gi * POOL_GROUP_DIM
        acc = u[:, c0:c0 + POOL_GROUP_DIM]
        for sft in range(1, wlen):
            acc = acc + ext[POOL_CARRY - sft:POOL_CARRY - sft + tm, c0:c0 + POOL_GROUP_DIM]
        mean = acc / jnp.minimum(float(wlen), pos)
        yg = (mean - u[:, c0:c0 + POOL_GROUP_DIM]).astype(BF16)
        zs.append(jnp.dot(yg, wg_ref[gi], preferred_element_type=F32))
    z = jnp.concatenate(zs, axis=1) * sc_ref[...]
    out = _layer_norm(ALPHA * x + z, g_ref[...], b_ref[...])
    y_ref[0] = out
    yb_ref[0] = out.astype(BF16)
    tail = ext[tm:tm + POOL_CARRY, :]
    st_ref[0] = tail
    ext[0:POOL_CARRY, :] = tail


def _pool_mixer_ln(x, hist, w_in, w_grp, scale, g, b, *, tm, pos0):
    bsz, s, _ = x.shape
    tm = min(tm, s)
    kern = functools.partial(_pool_kernel, tm=tm, pos0=pos0)
    blk = lambda bi, i: (bi, i, 0)
    per_b = lambda bi, i: (bi, 0, 0)
    fixed2 = lambda bi, i: (0, 0)
    return pl.pallas_call(
        kern,
        grid=(bsz, s // tm),
        in_specs=[pl.BlockSpec((1, tm, D_MODEL), blk),
                  pl.BlockSpec((1, POOL_CARRY, D_MODEL), per_b),
                  pl.BlockSpec((D_MODEL, D_MODEL), fixed2),
                  pl.BlockSpec((N_POOL_GROUPS, POOL_GROUP_DIM, POOL_GROUP_DIM),
                               lambda bi, i: (0, 0, 0)),
                  pl.BlockSpec((1, D_MODEL), fixed2),
                  pl.BlockSpec((1, D_MODEL), fixed2),
                  pl.BlockSpec((1, D_MODEL), fixed2)],
        out_specs=[pl.BlockSpec((1, tm, D_MODEL), blk),
                   pl.BlockSpec((1, tm, D_MODEL), blk),
                   pl.BlockSpec((1, POOL_CARRY, D_MODEL), per_b)],
        out_shape=[jax.ShapeDtypeStruct((bsz, s, D_MODEL), F32),
                   jax.ShapeDtypeStruct((bsz, s, D_MODEL), BF16),
                   jax.ShapeDtypeStruct((bsz, POOL_CARRY, D_MODEL), F32)],
        scratch_shapes=[pltpu.VMEM((POOL_CARRY + tm, D_MODEL), F32)],
        compiler_params=_cparams(("arbitrary", "arbitrary")),
        name="pool_mixer",
    )(x, hist, w_in, w_grp, scale, g, b)


def _extract_top(vals, n, dst_ref):
    for it in range(n):
        m = jnp.max(vals, axis=0, keepdims=True)
        dst_ref[it:it + 1, :] = m
        if it + 1 < n:
            vals = jnp.where(vals == m, -jnp.inf, vals)


def _route_kernel(x_ref, wq_ref, sk_ref, a_ref, b_ref, kap_ref, qt, r1, r2, cl, *, tb):
    qt[...] = lax.dot_general(wq_ref[...], x_ref[...], (((1,), (1,)), ((), ())),
                              preferred_element_type=F32)

    def head_body(h, carry):
        q1 = qt[pl.ds(pl.multiple_of(h * PEER_KEY_DIM, PEER_KEY_DIM), PEER_HALF), :]
        q2 = qt[pl.ds(pl.multiple_of(h * PEER_KEY_DIM + PEER_HALF, PEER_HALF), PEER_HALF), :]
        s1 = jnp.dot(sk_ref[h, 0], q1.astype(BF16), preferred_element_type=F32)
        s2 = jnp.dot(sk_ref[h, 1], q2.astype(BF16), preferred_element_type=F32)
        _extract_top(s1, PEER_TOPK, r1)
        _extract_top(s2, PEER_TOPK, r2)
        lo2 = r2[0:SUBLANES, :]
        tiles = [r1[a:a + 1, :] + lo2 for a in range(SUBLANES)]
        tiles.append(r1[0:1, :] + r2[SUBLANES:PEER_TOPK, :])
        tiles.append(r1[SUBLANES:PEER_TOPK, :] + r2[0:1, :])
        cand = jnp.concatenate(tiles, axis=0)
        _extract_top(cand, PEER_TOPK + 1, cl)
        c_top = cl[0:1, :]
        tau = 0.5 * (cl[PEER_TOPK - 1:PEER_TOPK, :] + cl[PEER_TOPK:PEER_TOPK + 1, :])
        zsum = jnp.sum(jnp.where(cand > tau, jnp.exp(cand - c_top), 0.0), axis=0, keepdims=True)
        inv_z = 1.0 / zsum
        a = jnp.where(s1 >= r1[PEER_TOPK - 1:PEER_TOPK, :], jnp.exp(s1 - r1[0:1, :]), 0.0)
        b = jnp.where(s2 >= r2[PEER_TOPK - 1:PEER_TOPK, :], jnp.exp(s2 - r2[0:1, :]), 0.0)
        a_ref[0, h] = a
        b_ref[0, h] = b * (0.5 * inv_z)
        kap_ref[0, h] = jnp.exp(tau - c_top) * (0.5 * inv_z)
        return carry

    lax.fori_loop(0, PEER_HEADS, head_body, 0, unroll=ROUTE_HEAD_UNROLL)


def _peer_route(xb, wq_t, sk, tb):
    t = xb.shape[0]
    tb = min(tb, t)
    kern = functools.partial(_route_kernel, tb=tb)
    return pl.pallas_call(
        kern,
        grid=(t // tb,),
        in_specs=[pl.BlockSpec((tb, D_MODEL), lambda i: (i, 0)),
                  pl.BlockSpec((PEER_HEADS * PEER_KEY_DIM, D_MODEL), lambda i: (0, 0)),
                  pl.BlockSpec((PEER_HEADS, 2, N_KEYS, PEER_HALF), lambda i: (0, 0, 0, 0))],
        out_specs=[pl.BlockSpec((1, PEER_HEADS, N_KEYS, tb), lambda i: (i, 0, 0, 0)),
                   pl.BlockSpec((1, PEER_HEADS, N_KEYS, tb), lambda i: (i, 0, 0, 0)),
                   pl.BlockSpec((1, PEER_HEADS, 1, tb), lambda i: (i, 0, 0, 0))],
        out_shape=[jax.ShapeDtypeStruct((t // tb, PEER_HEADS, N_KEYS, tb), F32),
                   jax.ShapeDtypeStruct((t // tb, PEER_HEADS, N_KEYS, tb), F32),
                   jax.ShapeDtypeStruct((t // tb, PEER_HEADS, 1, tb), F32)],
        scratch_shapes=[pltpu.VMEM((PEER_HEADS * PEER_KEY_DIM, tb), F32),
                        pltpu.VMEM((PEER_TOPK, tb), F32),
                        pltpu.VMEM((PEER_TOPK, tb), F32),
                        pltpu.VMEM((3 * SUBLANES, tb), F32)],
        compiler_params=_cparams(("arbitrary",)),
        name="peer_route",
    )(xb, wq_t, sk)


def _peer_kernel(x_ref, u_ref, vt_ref, a_ref, b_ref, kap_ref,
                 g_ref, bt_ref, y_ref, acc, ht, gt, xt, *, tb, eb, rtb):
    e = pl.program_id(1)
    ni = eb // N_KEYS
    nlt = tb // LANES
    inv_sqrt2 = 2.0 ** -0.5

    @pl.when(e == 0)
    def _():
        acc[...] = jnp.zeros_like(acc)
        for lt in range(nlt):
            l0 = lt * LANES
            xt[:, l0:l0 + LANES] = x_ref[l0:l0 + LANES, :].T.astype(BF16)

    cgw = min(MXU_COLS, tb)
    ncg = tb // cgw
    lpc = cgw // LANES

    def stage_a(half, slot, cg):
        c0 = cg * cgw
        ht[slot, :, c0:c0 + cgw] = jnp.dot(
            u_ref[half * eb:(half + 1) * eb, :], xt[:, c0:c0 + cgw],
            preferred_element_type=F32)

    def stage_b(slot, row0, cg):
        for ii in range(ni):
            for lt in range(cg * lpc, (cg + 1) * lpc):
                l0 = lt * LANES
                sb, r0 = divmod(l0, rtb)
                wsum = jnp.zeros((N_KEYS, LANES), F32)
                for h in range(PEER_HEADS):
                    p = (a_ref[sb, h, row0 + ii:row0 + ii + 1, r0:r0 + LANES]
                         * b_ref[sb, h, :, r0:r0 + LANES])
                    wsum = wsum + jnp.where(p > kap_ref[sb, h, :, r0:r0 + LANES], p, 0.0)
                hh = ht[slot, ii * N_KEYS:(ii + 1) * N_KEYS, l0:l0 + LANES]
                gval = wsum * (hh * (1.0 + lax.erf(hh * inv_sqrt2)))
                g0 = slot * eb + ii * N_KEYS
                gt[g0:g0 + N_KEYS, l0:l0 + LANES] = gval.astype(BF16)

    def stage_c(cg):
        c0 = cg * cgw
        for mc in range(D_MODEL // PEER_ROW_CHUNK):
            r0 = mc * PEER_ROW_CHUNK
            acc[r0:r0 + PEER_ROW_CHUNK, c0:c0 + cgw] += jnp.dot(
                vt_ref[0, r0:r0 + PEER_ROW_CHUNK, :], gt[:, c0:c0 + cgw],
                preferred_element_type=F32)

    for half in range(2):
        for cg in range(ncg):
            stage_a(half, half, cg)
    for cg in range(ncg):
        for half in range(2):
            stage_b(half, half * ni, cg)
        stage_c(cg)

    @pl.when(e == pl.num_programs(1) - 1)
    def _():
        for lt in range(nlt):
            l0 = lt * LANES
            z = ALPHA * x_ref[l0:l0 + LANES, :] + acc[:, l0:l0 + LANES].T
            y_ref[l0:l0 + LANES, :] = _layer_norm(z, g_ref[...], bt_ref[...])


def _peer_dense_ln(x, u, vt, a, b, kap, g, bt, *, tb, eb):
    t = x.shape[0]
    tb = min(tb, t)
    assert 2 * eb == SUBLANES * N_KEYS
    n_e = N_EXPERTS // (2 * eb)
    rtb = a.shape[-1]
    nsb = tb // rtb
    kern = functools.partial(_peer_kernel, tb=tb, eb=eb, rtb=rtb)
    row = lambda i, e: (i, 0)
    fixed = lambda i, e: (0, 0)
    per_tok = lambda i, e: (i, 0, 0, 0)
    return pl.pallas_call(
        kern,
        grid=(t // tb, n_e),
        in_specs=[pl.BlockSpec((tb, D_MODEL), row),
                  pl.BlockSpec((2 * eb, D_MODEL), lambda i, e: (e, 0)),
                  pl.BlockSpec((1, D_MODEL, 2 * eb), lambda i, e: (e, 0, 0)),
                  pl.BlockSpec((nsb, PEER_HEADS, SUBLANES, rtb), lambda i, e: (i, 0, e, 0)),
                  pl.BlockSpec((nsb, PEER_HEADS, N_KEYS, rtb), per_tok),
                  pl.BlockSpec((nsb, PEER_HEADS, 1, rtb), per_tok),
                  pl.BlockSpec((1, D_MODEL), fixed),
                  pl.BlockSpec((1, D_MODEL), fixed)],
        out_specs=pl.BlockSpec((tb, D_MODEL), row),
        out_shape=jax.ShapeDtypeStruct((t, D_MODEL), F32),
        scratch_shapes=[pltpu.VMEM((D_MODEL, tb), F32),
                        pltpu.VMEM((2, eb, tb), F32),
                        pltpu.VMEM((2 * eb, tb), BF16),
                        pltpu.VMEM((D_MODEL, tb), BF16)],
        compiler_params=_cparams(("arbitrary", "arbitrary")),
        name="peer_dense",
    )(x, u, vt, a, b, kap, g, bt)


def _ple_kernel(x_ref, p_ref, wg_ref, wp_ref, y_ref):
    x = x_ref[...]
    gate = jax.nn.sigmoid(jnp.dot(x.astype(BF16), wg_ref[...], preferred_element_type=F32))
    emb = jnp.dot(p_ref[...].astype(BF16), wp_ref[...], preferred_element_type=F32)
    y_ref[...] = x + gate * emb


def _ple(x, p, wg, wp, tm):
    t = x.shape[0]
    tm = min(tm, t)
    row = lambda i: (i, 0)
    fixed = lambda i: (0, 0)
    return pl.pallas_call(
        _ple_kernel,
        grid=(t // tm,),
        in_specs=[pl.BlockSpec((tm, D_MODEL), row),
                  pl.BlockSpec((tm, PLE_DIM), row),
                  pl.BlockSpec((D_MODEL, D_MODEL), fixed),
                  pl.BlockSpec((PLE_DIM, D_MODEL), fixed)],
        out_specs=pl.BlockSpec((tm, D_MODEL), row),
        out_shape=jax.ShapeDtypeStruct((t, D_MODEL), F32),
        compiler_params=_cparams(("arbitrary",)),
        name="ple",
    )(x, p, wg, wp)


def _channel_mixer_tiles(n_tokens):
    return dict(
        tm=ROW_BLOCK,
        route_tb=min(n_tokens, 2 * LANES),
        peer_tb=min(n_tokens, ROW_BLOCK),
        peer_eb=PEER_STEP_EXPERTS // 2,
    )


def _channel_mixer(x1, x1b, p, wts, *, tm, route_tb, peer_tb, peer_eb):
    a, b, kap = _peer_route(x1b, wts["wq_t"], wts["sk"], route_tb)
    x2 = _peer_dense_ln(x1, wts["u"], wts["vt"], a, b, kap, wts["g1"], wts["b1"],
                        tb=peer_tb, eb=peer_eb)
    return _ple(x2, p, wts["wg"], wts["wp"], tm)


def _attention_mixer(x, hist_k, hist_v, pos0, wts, *, tm, qb, sub):
    bsz, s, _ = x.shape
    xf = x.reshape(bsz * s, D_MODEL)
    q, k, v = (a.reshape(bsz, s, D_MODEL) for a in _qkv_proj(xf, wts["wqkv"], tm))
    if hist_k is None:
        o = _band_attention(q, 0, k, 0, k, 0, v, 0, v, 0, wts["rel_bias"],
                            qb=qb, sub=sub, pos0=pos0, seq_len=s, prev_is_shifted=True)
    else:
        sp = -(-s // CHUNK) * CHUNK
        q, k, v = (jnp.pad(a, ((0, 0), (0, sp - s), (0, 0))) for a in (q, k, v))
        o = _band_attention(q, 0, hist_k, 0, k, 0, hist_v, 0, v, 0, wts["rel_bias"],
                            qb=sp, sub=CHUNK, pos0=pos0, seq_len=s, prev_is_shifted=False)
        o = o[:, :s]
    x1, x1b = _proj_ln(o.reshape(bsz * s, D_MODEL), wts["wo"], xf, wts["g0"], wts["b0"], tm)
    return x1, x1b


def _new_kv(x_rows, wts, tm):
    return _matmul(x_rows, wts["wkv"], F32, tm)


def kernel(x_prompt, x_sample, cache_att_k, cache_att_v, state_pool, p_prompt, p_sample,
           att_w_qkv, att_w_o, att_rel_bias, pool_w_in, pool_w_grp, pool_scale,
           peer_w_query, peer_sub_keys, peer_u, peer_v, ln_g, ln_b, ple_w_proj, ple_w_gate):
    bp, sp_len, _ = x_prompt.shape
    bs, ss_len, _ = x_sample.shape
    tm = ROW_BLOCK
    cfg_p = _channel_mixer_tiles(bp * sp_len)
    cfg_s = _channel_mixer_tiles(bs * ss_len)
    xp, xs = x_prompt, x_sample
    kp_out, vp_out, ks_out, vs_out, pp_out, ps_out = [], [], [], [], [], []
    for i in range(DEPTH):
        j = i // 2
        wts = {
            "wq_t": peer_w_query[i].T.astype(BF16),
            "sk": peer_sub_keys[i].astype(BF16),
            "u": peer_u[i].astype(BF16),
            "vt": peer_v[i].astype(BF16).reshape(-1, PEER_STEP_EXPERTS, D_MODEL).transpose(0, 2, 1),
            "g0": ln_g[i, 0].reshape(1, D_MODEL), "b0": ln_b[i, 0].reshape(1, D_MODEL),
            "g1": ln_g[i, 1].reshape(1, D_MODEL), "b1": ln_b[i, 1].reshape(1, D_MODEL),
            "wg": ple_w_gate[i].astype(BF16),
            "wp": ple_w_proj[i].astype(BF16),
        }
        if i % 2 == 0:
            wts["wqkv"] = att_w_qkv[j].astype(BF16)
            wts["wkv"] = wts["wqkv"][:, D_MODEL:]
            wts["wo"] = att_w_o[j].astype(BF16)
            wts["rel_bias"] = att_rel_bias[j]
            keep = min(BAND_PAST, sp_len)
            x1p, x1pb = _attention_mixer(xp, None, None, 0, wts, tm=tm, qb=BAND_PAST, sub=2 * CHUNK)
            kv_p = _new_kv(xp[:, sp_len - keep:].reshape(bp * keep, D_MODEL), wts, tm)
            kv_p = kv_p.reshape(bp, keep, 2, N_HEADS, HEAD_DIM)
            kp_out.append(kv_p[:, :, 0])
            vp_out.append(kv_p[:, :, 1])
            hk = cache_att_k[j].astype(BF16).reshape(bs, -1, D_MODEL)
            hv = cache_att_v[j].astype(BF16).reshape(bs, -1, D_MODEL)
            pad = BAND_PAST - hk.shape[1]
            hk = jnp.pad(hk, ((0, 0), (pad, 0), (0, 0)))
            hv = jnp.pad(hv, ((0, 0), (pad, 0), (0, 0)))
            x1s, x1sb = _attention_mixer(xs, hk, hv, PAST_LEN, wts, tm=tm, qb=CHUNK, sub=CHUNK)
            kv_s = _new_kv(xs.reshape(bs * ss_len, D_MODEL), wts, tm)
            kv_s = kv_s.reshape(bs, ss_len, 2, N_HEADS, HEAD_DIM)
            ks_out.append(kv_s[:, :, 0])
            vs_out.append(kv_s[:, :, 1])
        else:
            w_in = pool_w_in[j].astype(BF16)
            w_grp = pool_w_grp[j].astype(BF16)
            scale = pool_scale[j].reshape(1, D_MODEL)
            zero_hist = jnp.zeros((bp, POOL_CARRY, D_MODEL), F32)
            x1p, x1pb, st_p = _pool_mixer_ln(xp, zero_hist, w_in, w_grp, scale, wts["g0"], wts["b0"],
                                            tm=tm, pos0=0)
            hist_s = jnp.pad(state_pool[j], ((0, 0), (POOL_CARRY - POOL_HIST, 0), (0, 0)))
            x1s, x1sb, st_s = _pool_mixer_ln(xs, hist_s, w_in, w_grp, scale, wts["g0"], wts["b0"],
                                            tm=tm, pos0=PAST_LEN)
            x1p = x1p.reshape(bp * sp_len, D_MODEL)
            x1pb = x1pb.reshape(bp * sp_len, D_MODEL)
            x1s = x1s.reshape(bs * ss_len, D_MODEL)
            x1sb = x1sb.reshape(bs * ss_len, D_MODEL)
            pp_out.append(st_p[:, POOL_CARRY - POOL_HIST:])
            ps_out.append(st_s[:, POOL_CARRY - POOL_HIST:])
        xp = _channel_mixer(x1p, x1pb, p_prompt[i].reshape(bp * sp_len, PLE_DIM), wts,
                            **cfg_p).reshape(bp, sp_len, D_MODEL)
        xs = _channel_mixer(x1s, x1sb, p_sample[i].reshape(bs * ss_len, PLE_DIM), wts,
                            **cfg_s).reshape(bs, ss_len, D_MODEL)
    return (xp, xs, jnp.stack(kp_out), jnp.stack(vp_out), jnp.stack(ks_out), jnp.stack(vs_out),
            jnp.stack(pp_out), jnp.stack(ps_out))
```
